```python
import jax, jax.numpy as jnp
from jax import lax
import numpy as np

D_MODEL = 4096
BATCH = 4
SEQ = 2048
DEPTH = 2
DEC_BATCH = 8
DEC_SEQ = 8
PAST_LEN = 16384
PAGE_SIZE = 128

HD_A = 128
D_A = D_MODEL // 2
H_A = D_A // HD_A
MOBA_BLOCK = 256
MOBA_TOPK = 3
MOBA_Q_CHUNK = 16
H_G = 4
DK_T = D_MODEL // 4
DV_T = D_MODEL // 2
DK_G = DK_T // H_G
DV_G = DV_T // H_G
GLA_RANK = 16
GLA_TAU = 16.0
GLA_CHUNK = 64
D_FF = 11008
CONV_W = 3
PLE_DIM = 256
LN_EPS = 1e-5
ALPHA = (2 * DEPTH) ** 0.25
BETA = (8 * DEPTH) ** -0.25
IN_SPLITS = (D_A, 2 * D_A, 3 * D_A, 3 * D_A + DK_T, 3 * D_A + 2 * DK_T,
             3 * D_A + 2 * DK_T + DV_T, 3 * D_A + 2 * DK_T + 2 * DV_T)
N_IN = 3 * D_A + 2 * DK_T + 2 * DV_T + GLA_RANK

kernel_name = 'moba_gla_gated_parallel_decoder_step'


def layer_norm(x, g, b):
    xf = x.astype(jnp.float32)
    mu = jnp.mean(xf, axis=-1, keepdims=True)
    var = jnp.mean(jnp.square(xf - mu), axis=-1, keepdims=True)
    return ((xf - mu) * lax.rsqrt(var + LN_EPS) * g.astype(jnp.float32) + b.astype(jnp.float32)).astype(x.dtype)


def head_norm(o, g):
    mu = jnp.mean(o, axis=-1, keepdims=True)
    var = jnp.mean(jnp.square(o - mu), axis=-1, keepdims=True)
    return (o - mu) * lax.rsqrt(var + LN_EPS) * g.astype(jnp.float32)


def moba_attention(q, k_all, v_all, q_pos):
    B, Q = q.shape[:2]
    nb = k_all.shape[1] // MOBA_BLOCK
    kb = k_all.reshape(B, nb, MOBA_BLOCK, H_A, HD_A)
    vb = v_all.reshape(B, nb, MOBA_BLOCK, H_A, HD_A)
    k_mean = jnp.mean(kb.astype(jnp.float32), axis=2)
    n_top = min(MOBA_TOPK, nb)
    b_ix = jnp.arange(B)[:, None, None, None]
    h_ix = jnp.arange(H_A)[None, None, :, None]
    offs = jnp.arange(MOBA_BLOCK, dtype=jnp.int32)
    scale = HD_A ** -0.5

    def attend(args):
        qc, pc = args
        qn = qc.shape[1]
        qblk = pc // MOBA_BLOCK
        gate_s = jnp.einsum('bqhd,bnhd->bqhn', qc.astype(jnp.float32), k_mean)
        is_past = jnp.arange(nb)[None, :] < qblk[:, None]
        gate_s = jnp.where(is_past[None, :, None, :], gate_s, -jnp.inf)
        _, top = lax.top_k(gate_s, n_top)
        own = jnp.broadcast_to(qblk[None, :, None, None], (B, qn, H_A, 1))
        blk = jnp.concatenate([top.astype(jnp.int32), own.astype(jnp.int32)], axis=-1)
        slot_ok = jnp.concatenate([jnp.arange(n_top)[None, :] < qblk[:, None],
                                   jnp.ones((qn, 1), bool)], axis=-1)
        kg = kb[b_ix, blk, :, h_ix]
        vg = vb[b_ix, blk, :, h_ix]
        kpos = blk[..., None] * MOBA_BLOCK + offs
        mask = slot_ok[None, :, None, :, None] & (kpos <= pc[None, :, None, None, None])
        s = jnp.einsum('bqhd,bqhnkd->bqhnk', qc, kg).astype(jnp.float32) * scale
        s = jnp.where(mask, s, -jnp.inf)
        w = jax.nn.softmax(s.reshape(B, qn, H_A, -1), axis=-1).reshape(s.shape)
        return jnp.einsum('bqhnk,bqhnkd->bqhd', w.astype(vg.dtype), vg)

    if Q <= MOBA_Q_CHUNK or Q % MOBA_Q_CHUNK:
        return attend((q, q_pos))
    nqc = Q // MOBA_Q_CHUNK
    qs = jnp.moveaxis(q.reshape(B, nqc, MOBA_Q_CHUNK, H_A, HD_A), 1, 0)
    ps = q_pos.reshape(nqc, MOBA_Q_CHUNK)
    out = lax.map(attend, (qs, ps))
    return jnp.moveaxis(out, 0, 1).reshape(B, Q, H_A, HD_A)


def gla_chunk(S, xs):
    q, k, v, g = xs
    C = q.shape[2]
    cum = jnp.cumsum(g, axis=2)
    causal = jnp.tril(jnp.ones((C, C), bool))
    diff = cum[:, :, :, None, :] - cum[:, :, None, :, :]
    decay = jnp.exp(jnp.where(causal[:, :, None], diff, -jnp.inf))
    scores = jnp.einsum('bhid,bhjd,bhijd->bhij', q, k, decay)
    o = jnp.einsum('bhid,bhde->bhie', q * jnp.exp(cum), S) + jnp.einsum('bhij,bhje->bhie', scores, v)
    last = cum[:, :, -1:, :]
    S_new = jnp.exp(last[:, :, 0, :, None]) * S + jnp.einsum('bhjd,bhje->bhde', k * jnp.exp(last - cum), v)
    return S_new, o


def gla(q, k, v, g, S0):
    B, T = q.shape[:2]
    c = GLA_CHUNK if T % GLA_CHUNK == 0 else T
    nc = T // c

    def to_chunks(a):
        a = a.astype(jnp.float32).transpose(0, 2, 1, 3).reshape(B, H_G, nc, c, a.shape[-1])
        return jnp.moveaxis(a, 2, 0)

    S, o = lax.scan(gla_chunk, S0.astype(jnp.float32), (to_chunks(q), to_chunks(k), to_chunks(v), to_chunks(g)))
    o = jnp.moveaxis(o, 0, 2).reshape(B, H_G, T, DV_G).transpose(0, 2, 1, 3)
    return o, S


def decoder_layer(x, p, past_k, past_v, s_gla, conv_buf, q_pos,
                  ln1_g, ln1_b, ln2_g, ln2_b, w_in, w_gla_a2, b_gla_a, gla_norm_g,
                  w_proj_a, w_proj_g, w_merge_gate, w_out, w_up, conv_w, conv_b, w_down,
                  ple_proj, ple_gate):
    B, T, _ = x.shape
    h = x @ w_in
    qa, ka, va, qg, kg, vg, rg, glr = jnp.split(h, IN_SPLITS, axis=-1)
    qa = qa.reshape(B, T, H_A, HD_A)
    ka = ka.reshape(B, T, H_A, HD_A)
    va = va.reshape(B, T, H_A, HD_A)
    if past_k is None:
        k_all, v_all = ka, va
    else:
        k_all = jnp.concatenate([past_k.astype(ka.dtype), ka], axis=1)
        v_all = jnp.concatenate([past_v.astype(va.dtype), va], axis=1)
    pad = (-k_all.shape[1]) % MOBA_BLOCK
    k_all = jnp.pad(k_all, ((0, 0), (0, pad), (0, 0), (0, 0)))
    v_all = jnp.pad(v_all, ((0, 0), (0, pad), (0, 0), (0, 0)))
    o_a = moba_attention(qa, k_all, v_all, q_pos).reshape(B, T, D_A)
    log_a = jax.nn.log_sigmoid((glr @ w_gla_a2 + b_gla_a).astype(jnp.float32)) / GLA_TAU
    o_g, s_new = gla(qg.reshape(B, T, H_G, DK_G) * (DK_G ** -0.5), kg.reshape(B, T, H_G, DK_G),
                     vg.reshape(B, T, H_G, DV_G), log_a.reshape(B, T, H_G, DK_G), s_gla)
    o_g = head_norm(o_g, gla_norm_g).reshape(B, T, DV_T).astype(x.dtype) * jax.nn.silu(rg)
    gate_a, gate_g = jnp.split(jax.nn.sigmoid(x @ w_merge_gate), 2, axis=-1)
    mix = (gate_a * (o_a @ w_proj_a) + gate_g * (o_g @ w_proj_g)) @ w_out
    x = layer_norm(ALPHA * x + mix, ln1_g, ln1_b)
    u = x @ w_up
    u_ext = jnp.concatenate([conv_buf.astype(u.dtype), u], axis=1)
    c = conv_b + sum(conv_w[j] * u_ext[:, j:j + T] for j in range(CONV_W))
    conv_new = u_ext[:, T:]
    c_gate, c_val = jnp.split(c, 2, axis=-1)
    f = (jax.nn.gelu(c_gate, approximate=False) * c_val) @ w_down
    ple = jax.nn.sigmoid(x @ ple_gate) * (p @ ple_proj)
    x = layer_norm(ALPHA * x + f + ple, ln2_g, ln2_b)
    return x, ka, va, s_new, conv_new


def setup_inputs(seed: int = 0) -> dict:
    key = jax.random.key(seed)
    ks = jax.random.split(key, 32)
    n_pages = PAST_LEN // PAGE_SIZE
    n_phys = (5 * DEC_BATCH * n_pages) // 4

    def nrm(k, shape, s):
        return jax.random.normal(k, shape, jnp.float32) * s

    perm = jax.random.permutation(ks[6], n_phys)[: DEC_BATCH * n_pages]
    return {
        'x_prompt': nrm(ks[0], (BATCH, SEQ, D_MODEL), 1.0),
        'x_sample': nrm(ks[1], (DEC_BATCH, DEC_SEQ, D_MODEL), 1.0),
        'cache_k': nrm(ks[2], (n_phys, DEPTH, PAGE_SIZE, H_A, HD_A), 1.0),
        'cache_v': nrm(ks[3], (n_phys, DEPTH, PAGE_SIZE, H_A, HD_A), 1.0),
        'state_gla': nrm(ks[4], (DEC_BATCH, DEPTH, H_G, DK_G, DV_G), 0.1),
        'state_conv': nrm(ks[5], (DEC_BATCH, DEPTH, CONV_W - 1, 2 * D_FF), 1.0),
        'page_table': perm.reshape(DEC_BATCH, n_pages).astype(jnp.int32),
        'p_prompt': nrm(ks[7], (DEPTH, BATCH, SEQ, PLE_DIM), 1.0),
        'p_sample': nrm(ks[8], (DEPTH, DEC_BATCH, DEC_SEQ, PLE_DIM), 1.0),
        'ln1_g': 1.0 + nrm(ks[9], (DEPTH, D_MODEL), 0.02),
        'ln1_b': nrm(ks[10], (DEPTH, D_MODEL), 0.02),
        'ln2_g': 1.0 + nrm(ks[11], (DEPTH, D_MODEL), 0.02),
        'ln2_b': nrm(ks[12], (DEPTH, D_MODEL), 0.02),
        'w_in': nrm(ks[13], (DEPTH, D_MODEL, N_IN), D_MODEL ** -0.5),
        'w_gla_a2': nrm(ks[14], (DEPTH, GLA_RANK, DK_T), GLA_RANK ** -0.5),
        'b_gla_a': nrm(ks[15], (DEPTH, DK_T), 0.1),
        'gla_norm_g': 1.0 + nrm(ks[16], (DEPTH, DV_G), 0.02),
        'w_proj_a': nrm(ks[17], (DEPTH, D_A, D_MODEL), BETA * D_A ** -0.5),
        'w_proj_g': nrm(ks[18], (DEPTH, DV_T, D_MODEL), BETA * DV_T ** -0.5),
        'w_merge_gate': nrm(ks[19], (DEPTH, D_MODEL, 2 * D_MODEL), D_MODEL ** -0.5),
        'w_out': nrm(ks[20], (DEPTH, D_MODEL, D_MODEL), BETA * D_MODEL ** -0.5),
        'w_up': nrm(ks[21], (DEPTH, D_MODEL, 2 * D_FF), D_MODEL ** -0.5),
        'conv_w': nrm(ks[22], (DEPTH, CONV_W, 2 * D_FF), CONV_W ** -0.5),
        'conv_b': nrm(ks[23], (DEPTH, 2 * D_FF), 0.01),
        'w_down': nrm(ks[24], (DEPTH, D_FF, D_MODEL), BETA * D_FF ** -0.5),
        'ple_proj': nrm(ks[25], (DEPTH, PLE_DIM, D_MODEL), PLE_DIM ** -0.5),
        'ple_gate': nrm(ks[26], (DEPTH, D_MODEL, D_MODEL), D_MODEL ** -0.5),
    }


def reference(x_prompt, x_sample, cache_k, cache_v, state_gla, state_conv, page_table,
              p_prompt, p_sample, ln1_g, ln1_b, ln2_g, ln2_b, w_in, w_gla_a2, b_gla_a,
              gla_norm_g, w_proj_a, w_proj_g, w_merge_gate, w_out, w_up, conv_w, conv_b,
              w_down, ple_proj, ple_gate):
    B, T = x_prompt.shape[:2]
    Bd, Td = x_sample.shape[:2]
    n_pages = page_table.shape[1]
    past_len = n_pages * PAGE_SIZE
    pos_prompt = jnp.arange(T, dtype=jnp.int32)
    pos_sample = past_len + jnp.arange(Td, dtype=jnp.int32)
    s0_prompt = jnp.zeros((B, H_G, DK_G, DV_G), jnp.float32)
    conv0_prompt = jnp.zeros((B, CONV_W - 1, 2 * D_FF), x_prompt.dtype)

    xp, xs = x_prompt, x_sample
    kp_l, vp_l, sp_l, cp_l = [], [], [], []
    ks_l, vs_l, ss_l, cs_l = [], [], [], []
    for i in range(DEPTH):
        w_i = (ln1_g[i], ln1_b[i], ln2_g[i], ln2_b[i], w_in[i], w_gla_a2[i], b_gla_a[i],
               gla_norm_g[i], w_proj_a[i], w_proj_g[i], w_merge_gate[i], w_out[i], w_up[i],
               conv_w[i], conv_b[i], w_down[i], ple_proj[i], ple_gate[i])
        xp, kp, vp, sp, cp = decoder_layer(xp, p_prompt[i], None, None, s0_prompt, conv0_prompt,
                                           pos_prompt, *w_i)
        past_k = cache_k[page_table, i].reshape(Bd, past_len, H_A, HD_A)
        past_v = cache_v[page_table, i].reshape(Bd, past_len, H_A, HD_A)
        xs, kn, vn, sn, cn = decoder_layer(xs, p_sample[i], past_k, past_v, state_gla[:, i],
                                           state_conv[:, i], pos_sample, *w_i)
        kp_l.append(kp); vp_l.append(vp); sp_l.append(sp); cp_l.append(cp)
        ks_l.append(kn); vs_l.append(vn); ss_l.append(sn); cs_l.append(cn)

    k_prompt = jnp.stack(kp_l, axis=1)
    v_prompt = jnp.stack(vp_l, axis=1)
    gla_prompt = jnp.stack(sp_l, axis=1)
    conv_prompt = jnp.stack(cp_l, axis=1)
    k_sample = jnp.stack(ks_l, axis=1)
    v_sample = jnp.stack(vs_l, axis=1)
    gla_sample = jnp.stack(ss_l, axis=1)
    conv_sample = jnp.stack(cs_l, axis=1)
    return (xp, xs, k_prompt, v_prompt, gla_prompt, conv_prompt, k_sample, v_sample, gla_sample, conv_sample)
```

```python
import functools

import jax
import jax.numpy as jnp
from jax import lax
from jax.experimental import pallas as pl
from jax.experimental.pallas import tpu as pltpu

HD_A = 128
MOBA_BLOCK = 256
MOBA_TOPK = 3
PAGE_SIZE = 128
H_G = 4
GLA_RANK = 16
GLA_TAU = 16.0
GLA_CHUNK = 64
CONV_W = 3
LN_EPS = 1e-5

LANES = 128
V7X_VMEM_LIMIT_BYTES = 56 * 1024 * 1024

NEG = -1e30
BF16 = jnp.bfloat16
F32 = jnp.float32

_NT = (((1,), (1,)), ((), ()))
_TN = (((0,), (0,)), ((), ()))


def _params(*sem):
    return pltpu.CompilerParams(dimension_semantics=sem, vmem_limit_bytes=V7X_VMEM_LIMIT_BYTES)


def _row_tile(m, pref):
    t = min(m, pref)
    assert m % t == 0, (m, t)
    return t


def _col_tile(n, pref, *offsets):
    t = (min(n, pref) // LANES) * LANES
    while t > LANES and (n % t or any(o % t for o in offsets)):
        t -= LANES
    assert t >= LANES and n % t == 0 and not any(o % t for o in offsets), (n, pref, offsets)
    return t


def _dot(a, b):
    return jnp.dot(a, b, preferred_element_type=F32)


def _mm_kernel(x_ref, w_ref, o_ref):
    o_ref[...] = _dot(x_ref[...], w_ref[...]).astype(o_ref.dtype)


def _matmul(x, w, col0, ncols, out_dtype, name):
    m, k = x.shape
    tm = _row_tile(m, 1024)
    tn = _col_tile(ncols, 512, col0)
    c0 = col0 // tn
    return pl.pallas_call(
        _mm_kernel,
        out_shape=jax.ShapeDtypeStruct((m, ncols), out_dtype),
        grid=(m // tm, ncols // tn),
        in_specs=[pl.BlockSpec((tm, k), lambda i, j: (i, 0)),
                  pl.BlockSpec((k, tn), lambda i, j: (0, c0 + j))],
        out_specs=pl.BlockSpec((tm, tn), lambda i, j: (i, j)),
        compiler_params=_params("parallel", "parallel"),
        name=name,
    )(x, w)


def _mix_kernel(x_ref, oa_ref, og_ref, wga_ref, wgg_ref, wpa_ref, wpg_ref, o_ref):
    x = x_ref[...]
    gate_a = jax.nn.sigmoid(_dot(x, wga_ref[...]))
    gate_g = jax.nn.sigmoid(_dot(x, wgg_ref[...]))
    mix = gate_a * _dot(oa_ref[...], wpa_ref[...]) + gate_g * _dot(og_ref[...], wpg_ref[...])
    o_ref[...] = mix.astype(o_ref.dtype)


def _gated_mix(xb, oa, og, w_gate, w_pa, w_pg):
    m, d = xb.shape
    da, dv = oa.shape[1], og.shape[1]
    tm = _row_tile(m, 1024)
    tn = _col_tile(d, 256)
    nj = d // tn
    return pl.pallas_call(
        _mix_kernel,
        out_shape=jax.ShapeDtypeStruct((m, d), BF16),
        grid=(m // tm, nj),
        in_specs=[pl.BlockSpec((tm, d), lambda i, j: (i, 0)),
                  pl.BlockSpec((tm, da), lambda i, j: (i, 0)),
                  pl.BlockSpec((tm, dv), lambda i, j: (i, 0)),
                  pl.BlockSpec((d, tn), lambda i, j: (0, j)),
                  pl.BlockSpec((d, tn), lambda i, j: (0, nj + j)),
                  pl.BlockSpec((da, tn), lambda i, j: (0, j)),
                  pl.BlockSpec((dv, tn), lambda i, j: (0, j))],
        out_specs=pl.BlockSpec((tm, tn), lambda i, j: (i, j)),
        compiler_params=_params("parallel", "parallel"),
        name="gated_mix",
    )(xb, oa, og, w_gate, w_gate, w_pa, w_pg)


def _mm_res_kernel(a_ref, w_ref, r_ref, o_ref, *, alpha):
    o_ref[...] = alpha * r_ref[...] + _dot(a_ref[...], w_ref[...])


def _matmul_residual(a, w, res, alpha):
    m, k = a.shape
    n = w.shape[1]
    tm = _row_tile(m, 1024)
    tn = _col_tile(n, 512)
    return pl.pallas_call(
        functools.partial(_mm_res_kernel, alpha=alpha),
        out_shape=jax.ShapeDtypeStruct((m, n), F32),
        grid=(m // tm, n // tn),
        in_specs=[pl.BlockSpec((tm, k), lambda i, j: (i, 0)),
                  pl.BlockSpec((k, tn), lambda i, j: (0, j)),
                  pl.BlockSpec((tm, tn), lambda i, j: (i, j))],
        out_specs=pl.BlockSpec((tm, tn), lambda i, j: (i, j)),
        compiler_params=_params("parallel", "parallel"),
        name="out_proj_residual",
    )(a, w, res)


def _ffn_out_kernel(act_ref, wd_ref, xb_ref, wg_ref, p_ref, wp_ref, r_ref, o_ref, *, alpha):
    f = _dot(act_ref[...], wd_ref[...])
    ple = jax.nn.sigmoid(_dot(xb_ref[...], wg_ref[...])) * _dot(p_ref[...], wp_ref[...])
    o_ref[...] = (alpha * r_ref[...] + f) + ple


def _ffn_out(act, w_down, xb, w_pgate, pb, w_pproj, res, alpha):
    m, f = act.shape
    d = w_down.shape[1]
    pd = pb.shape[1]
    tm = _row_tile(m, 512)
    tn = _col_tile(d, 256)
    return pl.pallas_call(
        functools.partial(_ffn_out_kernel, alpha=alpha),
        out_shape=jax.ShapeDtypeStruct((m, d), F32),
        grid=(m // tm, d // tn),
        in_specs=[pl.BlockSpec((tm, f), lambda i, j: (i, 0)),
                  pl.BlockSpec((f, tn), lambda i, j: (0, j)),
                  pl.BlockSpec((tm, d), lambda i, j: (i, 0)),
                  pl.BlockSpec((d, tn), lambda i, j: (0, j)),
                  pl.BlockSpec((tm, pd), lambda i, j: (i, 0)),
                  pl.BlockSpec((pd, tn), lambda i, j: (0, j)),
                  pl.BlockSpec((tm, tn), lambda i, j: (i, j))],
        out_specs=pl.BlockSpec((tm, tn), lambda i, j: (i, j)),
        compiler_params=_params("parallel", "parallel"),
        name="ffn_out",
    )(act, w_down, xb, w_pgate, pb, w_pproj, res)


def _ln_kernel(s_ref, g_ref, b_ref, o_ref, ob_ref):
    x = s_ref[...]
    mu = jnp.mean(x, axis=-1, keepdims=True)
    xc = x - mu
    var = jnp.mean(xc * xc, axis=-1, keepdims=True)
    y = xc * lax.rsqrt(var + LN_EPS) * g_ref[...] + b_ref[...]
    o_ref[...] = y
    ob_ref[...] = y.astype(ob_ref.dtype)


def _layer_norm(s, g, b):
    m, d = s.shape
    tm = _row_tile(m, 256)
    return pl.pallas_call(
        _ln_kernel,
        out_shape=(jax.ShapeDtypeStruct((m, d), F32), jax.ShapeDtypeStruct((m, d), BF16)),
        grid=(m // tm,),
        in_specs=[pl.BlockSpec((tm, d), lambda i: (i, 0)),
                  pl.BlockSpec((1, d), lambda i: (0, 0)),
                  pl.BlockSpec((1, d), lambda i: (0, 0))],
        out_specs=(pl.BlockSpec((tm, d), lambda i: (i, 0)), pl.BlockSpec((tm, d), lambda i: (i, 0))),
        compiler_params=_params("parallel"),
        name="layer_norm",
    )(s, g.reshape(1, d), b.reshape(1, d))


def _conv_gate_kernel(ug_ref, uv_ref, bg_ref, bv_ref, wg_ref, wv_ref, cg_ref, cv_ref, o_ref):
    def conv(u_ref, buf_ref, w_ref, cb_ref):
        u = u_ref[...]
        buf = buf_ref[...]
        w = w_ref[...]
        row = lax.broadcasted_iota(jnp.int32, u.shape, 0)
        u1 = jnp.where(row == 0, buf[1:2], pltpu.roll(u, 1, 0))
        u2 = jnp.where(row == 0, buf[0:1], jnp.where(row == 1, buf[1:2], pltpu.roll(u, 2, 0)))
        return cb_ref[...] + ((w[0:1] * u2 + w[1:2] * u1) + w[2:3] * u)

    c_gate = conv(ug_ref, bg_ref, wg_ref, cg_ref)
    c_val = conv(uv_ref, bv_ref, wv_ref, cv_ref)
    gelu = 0.5 * c_gate * (1.0 + lax.erf(c_gate * (2.0 ** -0.5)))
    o_ref[...] = (gelu * c_val).astype(o_ref.dtype)


def _conv_gate(u3, buf, conv_w, conv_b):
    bsz, t, f2 = u3.shape
    f = f2 // 2
    tc = _col_tile(f, 256)
    nj = f // tc
    cb = conv_b.reshape(1, f2)
    u_spec = lambda off: pl.BlockSpec((None, t, tc), lambda b, j: (b, 0, off + j))
    b_spec = lambda off: pl.BlockSpec((None, CONV_W - 1, tc), lambda b, j: (b, 0, off + j))
    w_spec = lambda off: pl.BlockSpec((CONV_W, tc), lambda b, j: (0, off + j))
    c_spec = lambda off: pl.BlockSpec((1, tc), lambda b, j: (0, off + j))
    return pl.pallas_call(
        _conv_gate_kernel,
        out_shape=jax.ShapeDtypeStruct((bsz, t, f), BF16),
        grid=(bsz, nj),
        in_specs=[u_spec(0), u_spec(nj), b_spec(0), b_spec(nj), w_spec(0), w_spec(nj), c_spec(0), c_spec(nj)],
        out_specs=pl.BlockSpec((None, t, tc), lambda b, j: (b, 0, j)),
        compiler_params=_params("parallel", "parallel"),
        name="conv_gate",
    )(u3, u3, buf, buf, conv_w, conv_w, cb, cb)


def _block_mean_kernel(k_ref, o_ref):
    o_ref[...] = jnp.sum(k_ref[...], axis=0, keepdims=True) * (1.0 / MOBA_BLOCK)


def _block_mean(k2d):
    m, da = k2d.shape
    nb = m // MOBA_BLOCK
    out = pl.pallas_call(
        _block_mean_kernel,
        out_shape=jax.ShapeDtypeStruct((nb, 1, da), F32),
        grid=(nb,),
        in_specs=[pl.BlockSpec((MOBA_BLOCK, da), lambda i: (i, 0))],
        out_specs=pl.BlockSpec((None, 1, da), lambda i: (i, 0, 0)),
        compiler_params=_params("parallel"),
        name="block_mean",
    )(k2d)
    return out.reshape(nb, da)


def _moba_prompt_kernel(q_ref, k_ref, v_ref, km_ref, o_ref, *, nb, scale):
    t = pl.program_id(2)
    blk = MOBA_BLOCK
    q = q_ref[...]
    kb = k_ref[...].astype(BF16)
    vb = v_ref[...].astype(BF16)

    g = lax.dot_general(q, km_ref[...].astype(BF16), _NT, preferred_element_type=F32)
    lane = lax.broadcasted_iota(jnp.int32, g.shape, 1)
    g = jnp.where(lane < t, g, -jnp.inf)
    tv = jnp.zeros((blk, 1), jnp.int32) + t

    s = lax.dot_general(q, kb, _NT, preferred_element_type=F32) * scale
    pieces = []
    for n in range(nb):
        gn = g[:, n:n + 1]
        beats = jnp.where(g > gn, 1.0, jnp.where((g == gn) & (lane < n), 1.0, 0.0))
        rank = jnp.sum(beats, axis=1, keepdims=True)
        bias = jnp.where((rank < MOBA_TOPK) & (tv > n), 0.0, NEG)
        pieces.append(s[:, n * blk:(n + 1) * blk] + bias)
    s_past = jnp.concatenate(pieces, axis=1)

    off = pl.multiple_of(t * blk, blk)
    k_own = k_ref[pl.ds(off, blk), :].astype(BF16)
    v_own = v_ref[pl.ds(off, blk), :].astype(BF16)
    s_own = lax.dot_general(q, k_own, _NT, preferred_element_type=F32) * scale
    row = lax.broadcasted_iota(jnp.int32, s_own.shape, 0)
    col = lax.broadcasted_iota(jnp.int32, s_own.shape, 1)
    s_own = jnp.where(col <= row, s_own, NEG)

    m = jnp.maximum(jnp.max(s_past, axis=1, keepdims=True), jnp.max(s_own, axis=1, keepdims=True))
    p_past = jnp.exp(s_past - m)
    p_own = jnp.exp(s_own - m)
    denom = jnp.sum(p_past, axis=1, keepdims=True) + jnp.sum(p_own, axis=1, keepdims=True)
    o = _dot(p_past.astype(BF16), vb) + _dot(p_own.astype(BF16), v_own)
    o_ref[...] = (o / denom).astype(o_ref.dtype)


def _moba_prompt(qb, k2d, v2d, bsz, t):
    m, da = qb.shape
    assert t % MOBA_BLOCK == 0
    h = da // HD_A
    nb = t // MOBA_BLOCK
    kmean = _block_mean(k2d)
    return pl.pallas_call(
        functools.partial(_moba_prompt_kernel, nb=nb, scale=HD_A ** -0.5),
        out_shape=jax.ShapeDtypeStruct((m, da), BF16),
        grid=(bsz, h, nb),
        in_specs=[pl.BlockSpec((MOBA_BLOCK, HD_A), lambda b, hh, tt: (b * nb + tt, hh)),
                  pl.BlockSpec((t, HD_A), lambda b, hh, tt: (b, hh)),
                  pl.BlockSpec((t, HD_A), lambda b, hh, tt: (b, hh)),
                  pl.BlockSpec((nb, HD_A), lambda b, hh, tt: (b, hh))],
        out_specs=pl.BlockSpec((MOBA_BLOCK, HD_A), lambda b, hh, tt: (b * nb + tt, hh)),
        compiler_params=_params("parallel", "parallel", "parallel"),
        name="moba_prompt",
    )(qb, k2d, v2d, kmean)


def _page_mean_kernel(pt_ref, a_ref, b_ref, o_ref):
    del pt_ref
    o_ref[...] = (jnp.sum(a_ref[...], axis=0, keepdims=True)
                  + jnp.sum(b_ref[...], axis=0, keepdims=True)) * (1.0 / MOBA_BLOCK)


def _page_block_mean(cache4, page_table, layer):
    _, _, psz, da = cache4.shape
    bd, n_pages = page_table.shape
    assert MOBA_BLOCK == 2 * psz and n_pages % 2 == 0
    nblk = n_pages // 2
    page_spec = lambda half: pl.BlockSpec(
        (None, None, psz, da), lambda b, j, pt: (pt[b, 2 * j + half], layer, 0, 0))
    out = pl.pallas_call(
        _page_mean_kernel,
        out_shape=jax.ShapeDtypeStruct((bd, nblk, 1, da), F32),
        grid_spec=pltpu.PrefetchScalarGridSpec(
            num_scalar_prefetch=1,
            grid=(bd, nblk),
            in_specs=[page_spec(0), page_spec(1)],
            out_specs=pl.BlockSpec((None, None, 1, da), lambda b, j, pt: (b, j, 0, 0))),
        compiler_params=_params("parallel", "parallel"),
        name="page_block_mean",
    )(page_table, cache4, cache4)
    return out.reshape(bd, nblk, da)


def _topk_kernel(q_ref, km_ref, o_ref, *, nblk):
    s = lax.dot_general(q_ref[...].astype(BF16), km_ref[...].astype(BF16), _NT,
                        preferred_element_type=F32)
    lane = lax.broadcasted_iota(jnp.int32, s.shape, 1).astype(F32)
    out_lane = lax.broadcasted_iota(jnp.int32, o_ref.shape, 1)
    out = jnp.zeros(o_ref.shape, jnp.int32)
    for r in range(MOBA_TOPK):
        mx = jnp.max(s, axis=1, keepdims=True)
        idx = jnp.min(jnp.where(s == mx, lane, float(nblk)), axis=1, keepdims=True)
        out = jnp.where(out_lane == r, idx.astype(jnp.int32), out)
        s = jnp.where(lane == idx, -jnp.inf, s)
    o_ref[...] = out


def _moba_topk(q2d, kmean, bd, td):
    _, nblk, da = kmean.shape
    assert nblk >= MOBA_TOPK
    h = da // HD_A
    out = pl.pallas_call(
        functools.partial(_topk_kernel, nblk=nblk),
        out_shape=jax.ShapeDtypeStruct((bd, h, td, LANES), jnp.int32),
        grid=(bd, h),
        in_specs=[pl.BlockSpec((td, HD_A), lambda b, hh: (b, hh)),
                  pl.BlockSpec((None, nblk, HD_A), lambda b, hh: (b, 0, hh))],
        out_specs=pl.BlockSpec((None, None, td, LANES), lambda b, hh: (b, hh, 0, 0)),
        compiler_params=_params("parallel", "parallel"),
        name="moba_topk",
    )(q2d, kmean)
    return out[..., :MOBA_TOPK]


def _moba_sample_kernel(pg_ref, q_ref, kn_ref, vn_ref, *refs, npg, scale):
    del pg_ref
    k_refs, v_refs, o_ref = refs[:npg], refs[npg:2 * npg], refs[2 * npg]
    qi = pl.program_id(2)
    q = q_ref[...].astype(BF16)
    kc = jnp.concatenate([r[...].astype(BF16) for r in k_refs], axis=0)
    vc = jnp.concatenate([r[...].astype(BF16) for r in v_refs], axis=0)
    s_past = lax.dot_general(q, kc, _NT, preferred_element_type=F32) * scale
    s_new = lax.dot_general(q, kn_ref[...].astype(BF16), _NT, preferred_element_type=F32) * scale
    row = lax.broadcasted_iota(jnp.int32, s_new.shape, 0)
    col = lax.broadcasted_iota(jnp.int32, s_new.shape, 1)
    s_new = jnp.where(col <= row, s_new, NEG)
    m = jnp.maximum(jnp.max(s_past, axis=1, keepdims=True), jnp.max(s_new, axis=1, keepdims=True))
    p_past = jnp.exp(s_past - m)
    p_new = jnp.exp(s_new - m)
    denom = jnp.sum(p_past, axis=1, keepdims=True) + jnp.sum(p_new, axis=1, keepdims=True)
    o = (_dot(p_past.astype(BF16), vc) + _dot(p_new.astype(BF16), vn_ref[...].astype(BF16))) / denom

    @pl.when(qi == 0)
    def _():
        o_ref[...] = jnp.zeros(o_ref.shape, o_ref.dtype)

    orow = lax.broadcasted_iota(jnp.int32, o_ref.shape, 0)
    o_ref[...] = jnp.where(orow == qi, o, o_ref[...])


def _moba_sample(q2d, kn2d, vn2d, cache_k4, cache_v4, page_table, layer, bd, td):
    _, _, psz, da = cache_k4.shape
    h = da // HD_A
    n_pages = page_table.shape[1]
    assert (n_pages * psz) % MOBA_BLOCK == 0 and td <= MOBA_BLOCK
    kmean = _page_block_mean(cache_k4, page_table, layer)
    top = _moba_topk(q2d, kmean, bd, td)
    ppb = MOBA_BLOCK // psz
    npg = MOBA_TOPK * ppb
    logical = (top[..., None] * ppb + jnp.arange(ppb, dtype=jnp.int32)).reshape(bd, h * td * npg)
    pages = jnp.take_along_axis(page_table, logical, axis=1).reshape(-1)

    def page_spec(j):
        return pl.BlockSpec((None, None, psz, HD_A),
                            lambda b, hh, qq, pg: (pg[((b * h + hh) * td + qq) * npg + j], layer, 0, hh))

    row_spec = pl.BlockSpec((td, HD_A), lambda b, hh, qq, pg: (b, hh))
    return pl.pallas_call(
        functools.partial(_moba_sample_kernel, npg=npg, scale=HD_A ** -0.5),
        out_shape=jax.ShapeDtypeStruct((bd * td, da), F32),
        grid_spec=pltpu.PrefetchScalarGridSpec(
            num_scalar_prefetch=1,
            grid=(bd, h, td),
            in_specs=[row_spec, row_spec, row_spec] + [page_spec(j) for j in range(npg)] * 2,
            out_specs=pl.BlockSpec((td, HD_A), lambda b, hh, qq, pg: (b, hh))),
        compiler_params=_params("parallel", "parallel", "arbitrary"),
        name="moba_sample",
    )(pages, q2d, kn2d, vn2d, *([cache_k4] * npg), *([cache_v4] * npg))


def _gla_kernel(q_ref, k_ref, v_ref, r_ref, glr_ref, w2_ref, b2_ref, ng_ref, s0_ref,
                o_ref, sn_ref, state, *, nc, hg, dk, dv):
    c = pl.program_id(1)
    csz = q_ref.shape[0]

    @pl.when(c == 0)
    def _():
        state[...] = s0_ref[...]

    z = _dot(glr_ref[...].astype(BF16), w2_ref[...]) + b2_ref[...]
    g = (jnp.minimum(z, 0.0) - jnp.log1p(jnp.exp(-jnp.abs(z)))) * (1.0 / GLA_TAU)
    row = lax.broadcasted_iota(jnp.int32, (csz, csz), 0)
    col = lax.broadcasted_iota(jnp.int32, (csz, csz), 1)
    causal = col <= row
    cum = jnp.dot(causal.astype(F32), g, precision=lax.Precision.HIGHEST, preferred_element_type=F32)
    ones = jnp.ones((csz, LANES), F32)

    for hh in range(hg):
        ks = slice(hh * dk, (hh + 1) * dk)
        vs = slice(hh * dv, (hh + 1) * dv)
        cum_h = cum[:, ks]
        last = cum_h[csz - 1:csz, :]
        q = q_ref[:, ks] * (dk ** -0.5)
        k = k_ref[:, ks]
        vb = v_ref[:, vs].astype(BF16)
        s_prev = state[hh]

        qe = (q * jnp.exp(cum_h)).astype(BF16)
        ke = (k * jnp.exp(-cum_h)).astype(BF16)
        a = lax.dot_general(qe, ke, _NT, preferred_element_type=F32)
        a = jnp.where(causal, a, 0.0)
        o = _dot(qe, s_prev.astype(BF16)) + _dot(a.astype(BF16), vb)

        kd = (k * jnp.exp(last - cum_h)).astype(BF16)
        tot = lax.dot_general(g[:, ks], ones, _TN, precision=lax.Precision.HIGHEST,
                              preferred_element_type=F32)
        state[hh] = jnp.exp(tot[:, :1]) * s_prev + lax.dot_general(kd, vb, _TN, preferred_element_type=F32)

        mu = jnp.mean(o, axis=-1, keepdims=True)
        oc = o - mu
        var = jnp.mean(oc * oc, axis=-1, keepdims=True)
        on = oc * lax.rsqrt(var + LN_EPS) * ng_ref[...]
        r = r_ref[:, vs]
        o_ref[:, vs] = (on * (r * jax.nn.sigmoid(r))).astype(o_ref.dtype)

    @pl.when(c == nc - 1)
    def _():
        sn_ref[...] = state[...]


def _gla(qg, kg, vg, rg, glr_pad, w2_pad, b2, norm_g, s0, bsz, t, out_dtype):
    m, dkt = qg.shape
    dvt = vg.shape[1]
    hg = s0.shape[1]
    dk, dv = dkt // hg, dvt // hg
    csz = GLA_CHUNK if t % GLA_CHUNK == 0 else t
    nc = t // csz
    rows = lambda w: pl.BlockSpec((csz, w), lambda b, c: (b * nc + c, 0))
    const = lambda shape: pl.BlockSpec(shape, lambda b, c: (0,) * len(shape))
    state_spec = pl.BlockSpec((None, hg, dk, dv), lambda b, c: (b, 0, 0, 0))
    return pl.pallas_call(
        functools.partial(_gla_kernel, nc=nc, hg=hg, dk=dk, dv=dv),
        out_shape=(jax.ShapeDtypeStruct((m, dvt), out_dtype), jax.ShapeDtypeStruct(s0.shape, F32)),
        grid=(bsz, nc),
        in_specs=[rows(dkt), rows(dkt), rows(dvt), rows(dvt), rows(LANES),
                  const((LANES, dkt)), const((1, dkt)), const((1, dv)), state_spec],
        out_specs=(rows(dvt), state_spec),
        scratch_shapes=[pltpu.VMEM((hg, dk, dv), F32)],
        compiler_params=_params("parallel", "arbitrary"),
        name="gla",
    )(qg, kg, vg, rg, glr_pad, w2_pad, b2.reshape(1, dkt), norm_g.reshape(1, dv), s0)


def _decoder_layer(x, xb, pb, bsz, t, w, moba_fn, s0, conv_buf, alpha, attn_dtype):
    d = x.shape[1]
    da, dkt, dvt = d // 2, d // 4, d // 2
    qa = _matmul(xb, w["w_in"], 0, da, attn_dtype, "in_proj_qa")
    ka = _matmul(xb, w["w_in"], da, da, F32, "in_proj_ka")
    va = _matmul(xb, w["w_in"], 2 * da, da, F32, "in_proj_va")
    off = 3 * da
    qg = _matmul(xb, w["w_in"], off, dkt, F32, "in_proj_qg")
    kg = _matmul(xb, w["w_in"], off + dkt, dkt, F32, "in_proj_kg")
    vg = _matmul(xb, w["w_in"], off + 2 * dkt, dvt, F32, "in_proj_vg")
    rg = _matmul(xb, w["w_in"], off + 2 * dkt + dvt, dvt, F32, "in_proj_rg")
    glr = _matmul(xb, w["w_glr"], 0, LANES, F32, "in_proj_glr")

    o_a = moba_fn(qa, ka, va).astype(BF16)
    o_g, s_new = _gla(qg, kg, vg, rg, glr, w["w_gla_a2"], w["b_gla_a"], w["gla_norm_g"], s0, bsz, t, attn_dtype)
    o_g = o_g.astype(BF16)

    mix = _gated_mix(xb, o_a, o_g, w["w_merge_gate"], w["w_proj_a"], w["w_proj_g"])
    x1, x1b = _layer_norm(_matmul_residual(mix, w["w_out"], x, alpha), w["ln1_g"], w["ln1_b"])

    u = _matmul(x1b, w["w_up"], 0, w["w_up"].shape[1], F32, "up_proj")
    u3 = u.reshape(bsz, t, -1)
    act = _conv_gate(u3, conv_buf, w["conv_w"], w["conv_b"]).reshape(bsz * t, -1)
    conv_new = u3[:, t - (CONV_W - 1):]
    s2 = _ffn_out(act, w["w_down"], x1b, w["ple_gate"], pb, w["ple_proj"], x1, alpha)
    x2, x2b = _layer_norm(s2, w["ln2_g"], w["ln2_b"])
    return x2, x2b, ka, va, s_new, conv_new


def kernel(x_prompt, x_sample, cache_k, cache_v, state_gla, state_conv, page_table, p_prompt, p_sample,
           ln1_g, ln1_b, ln2_g, ln2_b, w_in, w_gla_a2, b_gla_a, gla_norm_g, w_proj_a, w_proj_g,
           w_merge_gate, w_out, w_up, conv_w, conv_b, w_down, ple_proj, ple_gate):
    bp, tp, d = x_prompt.shape
    bd, td, _ = x_sample.shape
    depth = w_in.shape[0]
    n_phys, _, psz, h_a, hd = cache_k.shape
    assert hd == HD_A and psz == PAGE_SIZE
    hg, dk, dv = state_gla.shape[2:]
    f2 = w_up.shape[2]
    alpha = (2 * depth) ** 0.25
    n_main = w_in.shape[2] - GLA_RANK

    cache_k4 = cache_k.reshape(n_phys, depth, psz, h_a * hd)
    cache_v4 = cache_v.reshape(n_phys, depth, psz, h_a * hd)
    s0_prompt = jnp.zeros((bp, hg, dk, dv), F32)
    conv0_prompt = jnp.zeros((bp, CONV_W - 1, f2), F32)

    xp, xpb = x_prompt.reshape(bp * tp, d), x_prompt.reshape(bp * tp, d).astype(BF16)
    xs, xsb = x_sample.reshape(bd * td, d), x_sample.reshape(bd * td, d).astype(BF16)
    outs = [[] for _ in range(8)]
    for i in range(depth):
        w = {
            "w_in": w_in[i].astype(BF16),
            "w_glr": jnp.pad(w_in[i, :, n_main:], ((0, 0), (0, LANES - GLA_RANK))).astype(BF16),
            "w_gla_a2": jnp.pad(w_gla_a2[i], ((0, LANES - GLA_RANK), (0, 0))).astype(BF16),
            "b_gla_a": b_gla_a[i], "gla_norm_g": gla_norm_g[i],
            "w_proj_a": w_proj_a[i].astype(BF16), "w_proj_g": w_proj_g[i].astype(BF16),
            "w_merge_gate": w_merge_gate[i].astype(BF16), "w_out": w_out[i].astype(BF16),
            "w_up": w_up[i].astype(BF16), "conv_w": conv_w[i], "conv_b": conv_b[i],
            "w_down": w_down[i].astype(BF16), "ple_proj": ple_proj[i].astype(BF16),
            "ple_gate": ple_gate[i].astype(BF16),
            "ln1_g": ln1_g[i], "ln1_b": ln1_b[i], "ln2_g": ln2_g[i], "ln2_b": ln2_b[i],
        }
        moba_p = lambda q, k, v: _moba_prompt(q, k, v, bp, tp)
        xp, xpb, kp, vp, sp, cp = _decoder_layer(
            xp, xpb, p_prompt[i].reshape(bp * tp, -1).astype(BF16), bp, tp, w, moba_p,
            s0_prompt, conv0_prompt, alpha, BF16)
        moba_s = lambda q, k, v: _moba_sample(q, k, v, cache_k4, cache_v4, page_table, i, bd, td)
        xs, xsb, kn, vn, sn, cn = _decoder_layer(
            xs, xsb, p_sample[i].reshape(bd * td, -1).astype(BF16), bd, td, w, moba_s,
            state_gla[:, i], state_conv[:, i], alpha, F32)
        for lst, val in zip(outs, (kp.reshape(bp, tp, h_a, hd), vp.reshape(bp, tp, h_a, hd), sp, cp,
                                   kn.reshape(bd, td, h_a, hd), vn.reshape(bd, td, h_a, hd), sn, cn)):
            lst.append(val)

    stacked = [jnp.stack(lst, axis=1) for lst in outs]
    return (xp.reshape(bp, tp, d), xs.reshape(bd, td, d), *stacked)
```

```python
import functools

import jax
import jax.numpy as jnp
from jax import lax
from jax.experimental import pallas as pl
from jax.experimental.pallas import tpu as pltpu

HD_A = 128
MOBA_BLOCK = 256
MOBA_TOPK = 3
PAGE_SIZE = 128
H_G = 4
GLA_RANK = 16
GLA_TAU = 16.0
GLA_CHUNK = 64
CONV_W = 3
LN_EPS = 1e-5

LANES = 128
V7X_VMEM_LIMIT_BYTES = 56 * 1024 * 1024
CONV_BLOCK_ELEMS = 512 * 1024

NEG = -1e30
BF16 = jnp.bfloat16
F32 = jnp.float32

_NT = (((1,), (1,)), ((), ()))
_TN = (((0,), (0,)), ((), ()))


def _params(*sem):
    return pltpu.CompilerParams(dimension_semantics=sem, vmem_limit_bytes=V7X_VMEM_LIMIT_BYTES)


def _row_tile(m, pref):
    t = min(m, pref)
    assert m % t == 0, (m, t)
    return t


def _col_tile(n, pref, *offsets):
    t = (min(n, pref) // LANES) * LANES
    while t > LANES and (n % t or any(o % t for o in offsets)):
        t -= LANES
    assert t >= LANES and n % t == 0 and not any(o % t for o in offsets), (n, pref, offsets)
    return t


def _dot(a, b):
    return jnp.dot(a, b, preferred_element_type=F32)


def _mm_kernel(x_ref, w_ref, o_ref):
    o_ref[...] = _dot(x_ref[...], w_ref[...]).astype(o_ref.dtype)


def _wspec(k, tn, layer, col_block0=0):
    return pl.BlockSpec((None, k, tn), lambda i, j: (layer, 0, col_block0 + j))


def _matmul(x, w, layer, col0, ncols, out_dtype, name):
    m, k = x.shape
    tm = _row_tile(m, 1024)
    tn = _col_tile(ncols, 512, col0)
    return pl.pallas_call(
        _mm_kernel,
        out_shape=jax.ShapeDtypeStruct((m, ncols), out_dtype),
        grid=(m // tm, ncols // tn),
        in_specs=[pl.BlockSpec((tm, k), lambda i, j: (i, 0)), _wspec(k, tn, layer, col0 // tn)],
        out_specs=pl.BlockSpec((tm, tn), lambda i, j: (i, j)),
        compiler_params=_params("parallel", "parallel"),
        name=name,
    )(x, w)


def _mix_kernel(x_ref, oa_ref, og_ref, wga_ref, wgg_ref, wpa_ref, wpg_ref, o_ref):
    x = x_ref[...]
    gate_a = jax.nn.sigmoid(_dot(x, wga_ref[...]))
    gate_g = jax.nn.sigmoid(_dot(x, wgg_ref[...]))
    mix = gate_a * _dot(oa_ref[...], wpa_ref[...]) + gate_g * _dot(og_ref[...], wpg_ref[...])
    o_ref[...] = mix.astype(o_ref.dtype)


def _gated_mix(xb, oa, og, w_gate, w_pa, w_pg, layer):
    m, d = xb.shape
    da, dv = oa.shape[1], og.shape[1]
    tm = _row_tile(m, 1024)
    tn = _col_tile(d, 256)
    nj = d // tn
    return pl.pallas_call(
        _mix_kernel,
        out_shape=jax.ShapeDtypeStruct((m, d), BF16),
        grid=(m // tm, nj),
        in_specs=[pl.BlockSpec((tm, d), lambda i, j: (i, 0)),
                  pl.BlockSpec((tm, da), lambda i, j: (i, 0)),
                  pl.BlockSpec((tm, dv), lambda i, j: (i, 0)),
                  _wspec(d, tn, layer), _wspec(d, tn, layer, nj), _wspec(da, tn, layer), _wspec(dv, tn, layer)],
        out_specs=pl.BlockSpec((tm, tn), lambda i, j: (i, j)),
        compiler_params=_params("parallel", "parallel"),
        name="gated_mix",
    )(xb, oa, og, w_gate, w_gate, w_pa, w_pg)


def _mm_res_kernel(a_ref, w_ref, r_ref, o_ref, *, alpha):
    o_ref[...] = alpha * r_ref[...] + _dot(a_ref[...], w_ref[...])


def _matmul_residual(a, w, layer, res, alpha):
    m, k = a.shape
    n = w.shape[2]
    tm = _row_tile(m, 1024)
    tn = _col_tile(n, 512)
    return pl.pallas_call(
        functools.partial(_mm_res_kernel, alpha=alpha),
        out_shape=jax.ShapeDtypeStruct((m, n), F32),
        grid=(m // tm, n // tn),
        in_specs=[pl.BlockSpec((tm, k), lambda i, j: (i, 0)), _wspec(k, tn, layer),
                  pl.BlockSpec((tm, tn), lambda i, j: (i, j))],
        out_specs=pl.BlockSpec((tm, tn), lambda i, j: (i, j)),
        compiler_params=_params("parallel", "parallel"),
        name="out_proj_residual",
    )(a, w, res)


def _ffn_out_kernel(act_ref, wd_ref, xb_ref, wg_ref, p_ref, wp_ref, r_ref, o_ref, *, alpha):
    f = _dot(act_ref[...], wd_ref[...])
    ple = jax.nn.sigmoid(_dot(xb_ref[...], wg_ref[...])) * _dot(p_ref[...], wp_ref[...])
    o_ref[...] = (alpha * r_ref[...] + f) + ple


def _ffn_out(act, w_down, xb, w_pgate, pb, w_pproj, layer, res, alpha):
    m, f = act.shape
    d = w_down.shape[2]
    pd = pb.shape[2]
    tm = _row_tile(m, 512)
    tn = _col_tile(d, 256)
    return pl.pallas_call(
        functools.partial(_ffn_out_kernel, alpha=alpha),
        out_shape=jax.ShapeDtypeStruct((m, d), F32),
        grid=(m // tm, d // tn),
        in_specs=[pl.BlockSpec((tm, f), lambda i, j: (i, 0)), _wspec(f, tn, layer),
                  pl.BlockSpec((tm, d), lambda i, j: (i, 0)), _wspec(d, tn, layer),
                  pl.BlockSpec((None, tm, pd), lambda i, j: (layer, i, 0)), _wspec(pd, tn, layer),
                  pl.BlockSpec((tm, tn), lambda i, j: (i, j))],
        out_specs=pl.BlockSpec((tm, tn), lambda i, j: (i, j)),
        compiler_params=_params("parallel", "parallel"),
        name="ffn_out",
    )(act, w_down, xb, w_pgate, pb, w_pproj, res)


def _ln_kernel(s_ref, g_ref, b_ref, o_ref, ob_ref):
    x = s_ref[...]
    mu = jnp.mean(x, axis=-1, keepdims=True)
    xc = x - mu
    var = jnp.mean(xc * xc, axis=-1, keepdims=True)
    y = xc * lax.rsqrt(var + LN_EPS) * g_ref[...] + b_ref[...]
    o_ref[...] = y
    ob_ref[...] = y.astype(ob_ref.dtype)


def _layer_norm(s, g, b):
    m, d = s.shape
    tm = _row_tile(m, 256)
    return pl.pallas_call(
        _ln_kernel,
        out_shape=(jax.ShapeDtypeStruct((m, d), F32), jax.ShapeDtypeStruct((m, d), BF16)),
        grid=(m // tm,),
        in_specs=[pl.BlockSpec((tm, d), lambda i: (i, 0)),
                  pl.BlockSpec((1, d), lambda i: (0, 0)),
                  pl.BlockSpec((1, d), lambda i: (0, 0))],
        out_specs=(pl.BlockSpec((tm, d), lambda i: (i, 0)), pl.BlockSpec((tm, d), lambda i: (i, 0))),
        compiler_params=_params("parallel"),
        name="layer_norm",
    )(s, g.reshape(1, d), b.reshape(1, d))


def _conv_gate_kernel(ug_ref, uv_ref, bg_ref, bv_ref, wg_ref, wv_ref, cg_ref, cv_ref, o_ref):
    def conv(u_ref, buf_ref, w_ref, cb_ref):
        u = u_ref[...]
        buf = buf_ref[...]
        w = w_ref[...]
        row = lax.broadcasted_iota(jnp.int32, u.shape, 0)
        u1 = jnp.where(row == 0, buf[1:2], pltpu.roll(u, 1, 0))
        u2 = jnp.where(row == 0, buf[0:1], jnp.where(row == 1, buf[1:2], pltpu.roll(u, 2, 0)))
        return cb_ref[...] + ((w[0:1] * u2 + w[1:2] * u1) + w[2:3] * u)

    c_gate = conv(ug_ref, bg_ref, wg_ref, cg_ref)
    c_val = conv(uv_ref, bv_ref, wv_ref, cv_ref)
    gelu = 0.5 * c_gate * (1.0 + lax.erf(c_gate * (2.0 ** -0.5)))
    o_ref[...] = (gelu * c_val).astype(o_ref.dtype)


def _conv_gate(u3, buf, conv_w, conv_b):
    bsz, t, f2 = u3.shape
    f = f2 // 2
    tc = _col_tile(f, max(256, CONV_BLOCK_ELEMS // t))
    nj = f // tc
    cb = conv_b.reshape(1, f2)
    u_spec = lambda off: pl.BlockSpec((None, t, tc), lambda b, j: (b, 0, off + j))
    b_spec = lambda off: pl.BlockSpec((None, CONV_W - 1, tc), lambda b, j: (b, 0, off + j))
    w_spec = lambda off: pl.BlockSpec((CONV_W, tc), lambda b, j: (0, off + j))
    c_spec = lambda off: pl.BlockSpec((1, tc), lambda b, j: (0, off + j))
    return pl.pallas_call(
        _conv_gate_kernel,
        out_shape=jax.ShapeDtypeStruct((bsz, t, f), BF16),
        grid=(bsz, nj),
        in_specs=[u_spec(0), u_spec(nj), b_spec(0), b_spec(nj), w_spec(0), w_spec(nj), c_spec(0), c_spec(nj)],
        out_specs=pl.BlockSpec((None, t, tc), lambda b, j: (b, 0, j)),
        compiler_params=_params("parallel", "parallel"),
        name="conv_gate",
    )(u3, u3, buf, buf, conv_w, conv_w, cb, cb)


def _block_mean_kernel(k_ref, o_ref):
    o_ref[...] = jnp.sum(k_ref[...], axis=0, keepdims=True) * (1.0 / MOBA_BLOCK)


def _block_mean(k2d):
    m, da = k2d.shape
    nb = m // MOBA_BLOCK
    out = pl.pallas_call(
        _block_mean_kernel,
        out_shape=jax.ShapeDtypeStruct((nb, 1, da), F32),
        grid=(nb,),
        in_specs=[pl.BlockSpec((MOBA_BLOCK, da), lambda i: (i, 0))],
        out_specs=pl.BlockSpec((None, 1, da), lambda i: (i, 0, 0)),
        compiler_params=_params("parallel"),
        name="block_mean",
    )(k2d)
    return out.reshape(nb, da)


def _moba_prompt_kernel(q_ref, k_ref, v_ref, km_ref, o_ref, *, nb, scale):
    t = pl.program_id(2)
    blk = MOBA_BLOCK
    row = lax.broadcasted_iota(jnp.int32, (blk, blk), 0)
    col = lax.broadcasted_iota(jnp.int32, (blk, blk), 1)

    def attend(c):
        q = q_ref[...]
        off = c * blk
        s_own = lax.dot_general(q, k_ref[off:off + blk, :].astype(BF16), _NT, preferred_element_type=F32) * scale
        s_own = jnp.where(col <= row, s_own, NEG)
        m = jnp.max(s_own, axis=1, keepdims=True)
        if c > 0:
            s_past = lax.dot_general(q, k_ref[0:off, :].astype(BF16), _NT, preferred_element_type=F32) * scale
            if c > MOBA_TOPK:
                g = lax.dot_general(q, km_ref[...].astype(BF16), _NT, preferred_element_type=F32)
                lane = lax.broadcasted_iota(jnp.int32, g.shape, 1)
                g = jnp.where(lane < c, g, -jnp.inf)
                pieces = []
                for n in range(c):
                    gn = g[:, n:n + 1]
                    ahead = jnp.where(g > gn, 1.0, jnp.where((g == gn) & (lane < n), 1.0, 0.0))
                    rank = jnp.sum(ahead, axis=1, keepdims=True)
                    pieces.append(s_past[:, n * blk:(n + 1) * blk] + jnp.where(rank < MOBA_TOPK, 0.0, NEG))
                s_past = jnp.concatenate(pieces, axis=1)
            m = jnp.maximum(m, jnp.max(s_past, axis=1, keepdims=True))
        p_own = jnp.exp(s_own - m)
        denom = jnp.sum(p_own, axis=1, keepdims=True)
        o = _dot(p_own.astype(BF16), v_ref[off:off + blk, :].astype(BF16))
        if c > 0:
            p_past = jnp.exp(s_past - m)
            denom = denom + jnp.sum(p_past, axis=1, keepdims=True)
            o = o + _dot(p_past.astype(BF16), v_ref[0:off, :].astype(BF16))
        o_ref[...] = (o / denom).astype(o_ref.dtype)

    for c in range(nb):
        pl.when(t == c)(functools.partial(attend, c))


def _moba_prompt(qb, k2d, v2d, bsz, t):
    m, da = qb.shape
    assert t % MOBA_BLOCK == 0
    h = da // HD_A
    nb = t // MOBA_BLOCK
    kmean = _block_mean(k2d)
    return pl.pallas_call(
        functools.partial(_moba_prompt_kernel, nb=nb, scale=HD_A ** -0.5),
        out_shape=jax.ShapeDtypeStruct((m, da), BF16),
        grid=(bsz, h, nb),
        in_specs=[pl.BlockSpec((MOBA_BLOCK, HD_A), lambda b, hh, tt: (b * nb + tt, hh)),
                  pl.BlockSpec((t, HD_A), lambda b, hh, tt: (b, hh)),
                  pl.BlockSpec((t, HD_A), lambda b, hh, tt: (b, hh)),
                  pl.BlockSpec((nb, HD_A), lambda b, hh, tt: (b, hh))],
        out_specs=pl.BlockSpec((MOBA_BLOCK, HD_A), lambda b, hh, tt: (b * nb + tt, hh)),
        compiler_params=_params("parallel", "parallel", "parallel"),
        name="moba_prompt",
    )(qb, k2d, v2d, kmean)


def _page_mean_kernel(pt_ref, a_ref, b_ref, o_ref):
    del pt_ref
    o_ref[...] = (jnp.sum(a_ref[...], axis=0) + jnp.sum(b_ref[...], axis=0)) * (1.0 / MOBA_BLOCK)


def _page_block_mean(cache, page_table, layer):
    _, _, psz, h, hd = cache.shape
    bd, n_pages = page_table.shape
    assert MOBA_BLOCK == 2 * psz and n_pages % 2 == 0
    nblk = n_pages // 2
    page_spec = lambda half: pl.BlockSpec(
        (None, None, psz, h, hd), lambda b, j, pt: (pt[b, 2 * j + half], layer, 0, 0, 0))
    out = pl.pallas_call(
        _page_mean_kernel,
        out_shape=jax.ShapeDtypeStruct((bd, nblk, h, hd), F32),
        grid_spec=pltpu.PrefetchScalarGridSpec(
            num_scalar_prefetch=1,
            grid=(bd, nblk),
            in_specs=[page_spec(0), page_spec(1)],
            out_specs=pl.BlockSpec((None, None, h, hd), lambda b, j, pt: (b, j, 0, 0))),
        compiler_params=_params("parallel", "parallel"),
        name="page_block_mean",
    )(page_table, cache, cache)
    return out.reshape(bd, nblk, h * hd)


def _topk_kernel(q_ref, km_ref, o_ref, *, nblk):
    s = lax.dot_general(q_ref[...].astype(BF16), km_ref[...].astype(BF16), _NT,
                        preferred_element_type=F32)
    lane = lax.broadcasted_iota(jnp.int32, s.shape, 1).astype(F32)
    out_lane = lax.broadcasted_iota(jnp.int32, o_ref.shape, 1)
    out = jnp.zeros(o_ref.shape, jnp.int32)
    for r in range(MOBA_TOPK):
        mx = jnp.max(s, axis=1, keepdims=True)
        idx = jnp.min(jnp.where(s == mx, lane, float(nblk)), axis=1, keepdims=True)
        out = jnp.where(out_lane == r, idx.astype(jnp.int32), out)
        s = jnp.where(lane == idx, -jnp.inf, s)
    o_ref[...] = out


def _moba_topk(q2d, kmean, bd, td):
    _, nblk, da = kmean.shape
    assert nblk >= MOBA_TOPK
    h = da // HD_A
    out = pl.pallas_call(
        functools.partial(_topk_kernel, nblk=nblk),
        out_shape=jax.ShapeDtypeStruct((bd, h, td, LANES), jnp.int32),
        grid=(bd, h),
        in_specs=[pl.BlockSpec((td, HD_A), lambda b, hh: (b, hh)),
                  pl.BlockSpec((None, nblk, HD_A), lambda b, hh: (b, 0, hh))],
        out_specs=pl.BlockSpec((None, None, td, LANES), lambda b, hh: (b, hh, 0, 0)),
        compiler_params=_params("parallel", "parallel"),
        name="moba_topk",
    )(q2d, kmean)
    return out[..., :MOBA_TOPK]


def _moba_sample_kernel(pg_ref, q_ref, kn_ref, vn_ref, ck_hbm, cv_hbm, o_ref, kbuf, vbuf, sem,
                        *, layer, nheads, npg, scale):
    td = q_ref.shape[0]
    n = td * npg
    psz = kbuf.shape[2]
    step = pl.program_id(0) * nheads + pl.program_id(1)
    nsteps = pl.num_programs(0) * nheads
    slot = step % 2

    def page_copies(page, head, half, i):
        return (pltpu.make_async_copy(ck_hbm.at[page, layer, :, head, :], kbuf.at[half, i], sem.at[0, half]),
                pltpu.make_async_copy(cv_hbm.at[page, layer, :, head, :], vbuf.at[half, i], sem.at[1, half]))

    def start_step(s, half):
        head = s % nheads
        for i in range(n):
            for cp in page_copies(pg_ref[s * n + i], head, half, i):
                cp.start()

    @pl.when(step == 0)
    def _():
        start_step(step, slot)

    @pl.when(step + 1 < nsteps)
    def _():
        start_step(step + 1, 1 - slot)

    for i in range(n):
        for cp in page_copies(0, 0, slot, i):
            cp.wait()

    q = q_ref[...].astype(BF16)
    kc = kbuf[slot].reshape(n * psz, HD_A).astype(BF16)
    vc = vbuf[slot].reshape(n * psz, HD_A).astype(BF16)
    s_past = lax.dot_general(q, kc, _NT, preferred_element_type=F32) * scale
    prow = lax.broadcasted_iota(jnp.int32, s_past.shape, 0)
    pcol = lax.broadcasted_iota(jnp.int32, s_past.shape, 1)
    own = (pcol >= prow * (npg * psz)) & (pcol < (prow + 1) * (npg * psz))
    s_past = jnp.where(own, s_past, NEG)
    s_new = lax.dot_general(q, kn_ref[...].astype(BF16), _NT, preferred_element_type=F32) * scale
    row = lax.broadcasted_iota(jnp.int32, s_new.shape, 0)
    col = lax.broadcasted_iota(jnp.int32, s_new.shape, 1)
    s_new = jnp.where(col <= row, s_new, NEG)
    m = jnp.maximum(jnp.max(s_past, axis=1, keepdims=True), jnp.max(s_new, axis=1, keepdims=True))
    p_past = jnp.exp(s_past - m)
    p_new = jnp.exp(s_new - m)
    denom = jnp.sum(p_past, axis=1, keepdims=True) + jnp.sum(p_new, axis=1, keepdims=True)
    o_ref[...] = (_dot(p_past.astype(BF16), vc) + _dot(p_new.astype(BF16), vn_ref[...].astype(BF16))) / denom


def _moba_sample(q2d, kn2d, vn2d, cache_k, cache_v, page_table, layer, bd, td):
    _, _, psz, h, hd = cache_k.shape
    da = h * hd
    n_pages = page_table.shape[1]
    assert (n_pages * psz) % MOBA_BLOCK == 0 and td <= MOBA_BLOCK
    kmean = _page_block_mean(cache_k, page_table, layer)
    top = _moba_topk(q2d, kmean, bd, td)
    ppb = MOBA_BLOCK // psz
    npg = MOBA_TOPK * ppb
    logical = (top[..., None] * ppb + jnp.arange(ppb, dtype=jnp.int32)).reshape(bd, h * td * npg)
    pages = jnp.take_along_axis(page_table, logical, axis=1).reshape(-1)

    row_spec = pl.BlockSpec((td, HD_A), lambda b, hh, pg: (b, hh))
    hbm_spec = pl.BlockSpec(memory_space=pl.ANY)
    return pl.pallas_call(
        functools.partial(_moba_sample_kernel, layer=layer, nheads=h, npg=npg, scale=HD_A ** -0.5),
        out_shape=jax.ShapeDtypeStruct((bd * td, da), F32),
        grid_spec=pltpu.PrefetchScalarGridSpec(
            num_scalar_prefetch=1,
            grid=(bd, h),
            in_specs=[row_spec, row_spec, row_spec, hbm_spec, hbm_spec],
            out_specs=pl.BlockSpec((td, HD_A), lambda b, hh, pg: (b, hh)),
            scratch_shapes=[pltpu.VMEM((2, td * npg, psz, HD_A), F32),
                            pltpu.VMEM((2, td * npg, psz, HD_A), F32),
                            pltpu.SemaphoreType.DMA((2, 2))]),
        compiler_params=_params("arbitrary", "arbitrary"),
        name="moba_sample",
    )(pages, q2d, kn2d, vn2d, cache_k, cache_v)


def _gla_kernel(q_ref, k_ref, v_ref, r_ref, glr_ref, w2_ref, b2_ref, ng_ref, s0_ref,
                o_ref, sn_ref, state, *, nc, hg, dk, dv):
    c = pl.program_id(1)
    csz = q_ref.shape[0]

    @pl.when(c == 0)
    def _():
        state[...] = s0_ref[...]

    z = _dot(glr_ref[...].astype(BF16), w2_ref[...]) + b2_ref[...]
    g = (jnp.minimum(z, 0.0) - jnp.log1p(jnp.exp(-jnp.abs(z)))) * (1.0 / GLA_TAU)
    row = lax.broadcasted_iota(jnp.int32, (csz, csz), 0)
    col = lax.broadcasted_iota(jnp.int32, (csz, csz), 1)
    causal = col <= row
    cum = jnp.dot(causal.astype(F32), g, precision=lax.Precision.HIGHEST, preferred_element_type=F32)
    ones = jnp.ones((csz, LANES), F32)

    for hh in range(hg):
        ks = slice(hh * dk, (hh + 1) * dk)
        vs = slice(hh * dv, (hh + 1) * dv)
        cum_h = cum[:, ks]
        last = cum_h[csz - 1:csz, :]
        q = q_ref[:, ks] * (dk ** -0.5)
        k = k_ref[:, ks]
        vb = v_ref[:, vs].astype(BF16)
        s_prev = state[hh]

        qe = (q * jnp.exp(cum_h)).astype(BF16)
        ke = (k * jnp.exp(-cum_h)).astype(BF16)
        a = lax.dot_general(qe, ke, _NT, preferred_element_type=F32)
        a = jnp.where(causal, a, 0.0)
        o = _dot(qe, s_prev.astype(BF16)) + _dot(a.astype(BF16), vb)

        kd = (k * jnp.exp(last - cum_h)).astype(BF16)
        tot = lax.dot_general(g[:, ks], ones, _TN, precision=lax.Precision.HIGHEST,
                              preferred_element_type=F32)
        state[hh] = jnp.exp(tot[:, :1]) * s_prev + lax.dot_general(kd, vb, _TN, preferred_element_type=F32)

        mu = jnp.mean(o, axis=-1, keepdims=True)
        oc = o - mu
        var = jnp.mean(oc * oc, axis=-1, keepdims=True)
        on = oc * lax.rsqrt(var + LN_EPS) * ng_ref[...]
        r = r_ref[:, vs]
        o_ref[:, vs] = (on * (r * jax.nn.sigmoid(r))).astype(o_ref.dtype)

    @pl.when(c == nc - 1)
    def _():
        sn_ref[...] = state[...]


def _gla(qg, kg, vg, rg, glr_pad, w2_pad, layer, b2, norm_g, s0, s0_layer, bsz, t, out_dtype):
    m, dkt = qg.shape
    dvt = vg.shape[1]
    hg = s0.shape[2]
    dk, dv = dkt // hg, dvt // hg
    csz = GLA_CHUNK if t % GLA_CHUNK == 0 else t
    nc = t // csz
    rows = lambda w: pl.BlockSpec((csz, w), lambda b, c: (b * nc + c, 0))
    const = lambda shape: pl.BlockSpec(shape, lambda b, c: (0,) * len(shape))
    return pl.pallas_call(
        functools.partial(_gla_kernel, nc=nc, hg=hg, dk=dk, dv=dv),
        out_shape=(jax.ShapeDtypeStruct((m, dvt), out_dtype), jax.ShapeDtypeStruct((bsz, hg, dk, dv), F32)),
        grid=(bsz, nc),
        in_specs=[rows(dkt), rows(dkt), rows(dvt), rows(dvt), rows(LANES),
                  pl.BlockSpec((None, LANES, dkt), lambda b, c: (layer, 0, 0)), const((1, dkt)), const((1, dv)),
                  pl.BlockSpec((None, None, hg, dk, dv), lambda b, c: (b, s0_layer, 0, 0, 0))],
        out_specs=(rows(dvt), pl.BlockSpec((None, hg, dk, dv), lambda b, c: (b, 0, 0, 0))),
        scratch_shapes=[pltpu.VMEM((hg, dk, dv), F32)],
        compiler_params=_params("parallel", "arbitrary"),
        name="gla",
    )(qg, kg, vg, rg, glr_pad, w2_pad, b2.reshape(1, dkt), norm_g.reshape(1, dv), s0)


def _decoder_layer(x, xb, pb, bsz, t, w, i, moba_fn, s0, s0_layer, conv_buf, alpha, attn_dtype):
    d = x.shape[1]
    da, dkt, dvt = d // 2, d // 4, d // 2
    qa = _matmul(xb, w["w_in"], i, 0, da, attn_dtype, "in_proj_qa")
    ka = _matmul(xb, w["w_in"], i, da, da, F32, "in_proj_ka")
    va = _matmul(xb, w["w_in"], i, 2 * da, da, F32, "in_proj_va")
    off = 3 * da
    qg = _matmul(xb, w["w_in"], i, off, dkt, F32, "in_proj_qg")
    kg = _matmul(xb, w["w_in"], i, off + dkt, dkt, F32, "in_proj_kg")
    vg = _matmul(xb, w["w_in"], i, off + 2 * dkt, dvt, F32, "in_proj_vg")
    rg = _matmul(xb, w["w_in"], i, off + 2 * dkt + dvt, dvt, F32, "in_proj_rg")
    glr = _matmul(xb, w["w_glr"], i, 0, LANES, F32, "in_proj_glr")

    o_a = moba_fn(qa, ka, va).astype(BF16)
    o_g, s_new = _gla(qg, kg, vg, rg, glr, w["w_gla_a2"], i, w["b_gla_a"][i], w["gla_norm_g"][i],
                      s0, s0_layer, bsz, t, attn_dtype)
    o_g = o_g.astype(BF16)

    mix = _gated_mix(xb, o_a, o_g, w["w_merge_gate"], w["w_proj_a"], w["w_proj_g"], i)
    x1, x1b = _layer_norm(_matmul_residual(mix, w["w_out"], i, x, alpha), w["ln1_g"][i], w["ln1_b"][i])

    u = _matmul(x1b, w["w_up"], i, 0, w["w_up"].shape[2], F32, "up_proj")
    u3 = u.reshape(bsz, t, -1)
    act = _conv_gate(u3, conv_buf, w["conv_w"][i], w["conv_b"][i]).reshape(bsz * t, -1)
    conv_new = u3[:, t - (CONV_W - 1):]
    s2 = _ffn_out(act, w["w_down"], x1b, w["ple_gate"], pb, w["ple_proj"], i, x1, alpha)
    x2, x2b = _layer_norm(s2, w["ln2_g"][i], w["ln2_b"][i])
    return x2, x2b, ka, va, s_new, conv_new


def kernel(x_prompt, x_sample, cache_k, cache_v, state_gla, state_conv, page_table, p_prompt, p_sample,
           ln1_g, ln1_b, ln2_g, ln2_b, w_in, w_gla_a2, b_gla_a, gla_norm_g, w_proj_a, w_proj_g,
           w_merge_gate, w_out, w_up, conv_w, conv_b, w_down, ple_proj, ple_gate):
    bp, tp, d = x_prompt.shape
    bd, td, _ = x_sample.shape
    depth = w_in.shape[0]
    _, _, psz, h_a, hd = cache_k.shape
    assert hd == HD_A and psz == PAGE_SIZE
    hg, dk, dv = state_gla.shape[2:]
    f2 = w_up.shape[2]
    alpha = (2 * depth) ** 0.25
    n_main = w_in.shape[2] - GLA_RANK

    s0_prompt = jnp.zeros((bp, 1, hg, dk, dv), F32)
    conv0_prompt = jnp.zeros((bp, CONV_W - 1, f2), F32)

    w = {
        "w_in": w_in.astype(BF16),
        "w_glr": jnp.pad(w_in[:, :, n_main:], ((0, 0), (0, 0), (0, LANES - GLA_RANK))).astype(BF16),
        "w_gla_a2": jnp.pad(w_gla_a2, ((0, 0), (0, LANES - GLA_RANK), (0, 0))).astype(BF16),
        "b_gla_a": b_gla_a, "gla_norm_g": gla_norm_g,
        "w_proj_a": w_proj_a.astype(BF16), "w_proj_g": w_proj_g.astype(BF16),
        "w_merge_gate": w_merge_gate.astype(BF16), "w_out": w_out.astype(BF16),
        "w_up": w_up.astype(BF16), "conv_w": conv_w, "conv_b": conv_b,
        "w_down": w_down.astype(BF16), "ple_proj": ple_proj.astype(BF16), "ple_gate": ple_gate.astype(BF16),
        "ln1_g": ln1_g, "ln1_b": ln1_b, "ln2_g": ln2_g, "ln2_b": ln2_b,
    }
    pb_prompt = p_prompt.reshape(depth, bp * tp, -1).astype(BF16)
    pb_sample = p_sample.reshape(depth, bd * td, -1).astype(BF16)

    xp, xpb = x_prompt.reshape(bp * tp, d), x_prompt.reshape(bp * tp, d).astype(BF16)
    xs, xsb = x_sample.reshape(bd * td, d), x_sample.reshape(bd * td, d).astype(BF16)
    outs = [[] for _ in range(8)]
    for i in range(depth):
        moba_p = lambda q, k, v: _moba_prompt(q, k, v, bp, tp)
        xp, xpb, kp, vp, sp, cp = _decoder_layer(
            xp, xpb, pb_prompt, bp, tp, w, i, moba_p, s0_prompt, 0, conv0_prompt, alpha, BF16)
        moba_s = lambda q, k, v: _moba_sample(q, k, v, cache_k, cache_v, page_table, i, bd, td)
        xs, xsb, kn, vn, sn, cn = _decoder_layer(
            xs, xsb, pb_sample, bd, td, w, i, moba_s, state_gla, i, state_conv[:, i], alpha, F32)
        for lst, val in zip(outs, (kp.reshape(bp, tp, h_a, hd), vp.reshape(bp, tp, h_a, hd), sp, cp,
                                   kn.reshape(bd, td, h_a, hd), vn.reshape(bd, td, h_a, hd), sn, cn)):
            lst.append(val)

    stacked = [jnp.stack(lst, axis=1) for lst in outs]
    return (xp.reshape(bp, tp, d), xs.reshape(bd, td, d), *stacked)
```

```python
import functools

import jax
import jax.numpy as jnp
from jax import lax
from jax.experimental import pallas as pl
from jax.experimental.pallas import tpu as pltpu

HD_A = 128
MOBA_BLOCK = 256
MOBA_TOPK = 3
PAGE_SIZE = 128
H_G = 4
GLA_RANK = 16
GLA_TAU = 16.0
GLA_CHUNK = 64
CONV_W = 3
LN_EPS = 1e-5

LANES = 128
BF16_SUBLANES = 16
V7X_VMEM_LIMIT_BYTES = 56 * 1024 * 1024
CONV_BLOCK_ELEMS = 512 * 1024
UP_CONV_ROW_CHUNK = 512

NEG = -1e30
BF16 = jnp.bfloat16
F32 = jnp.float32

_NT = (((1,), (1,)), ((), ()))
_TN = (((0,), (0,)), ((), ()))


def _params(*sem):
    return pltpu.CompilerParams(dimension_semantics=sem, vmem_limit_bytes=V7X_VMEM_LIMIT_BYTES)


def _row_tile(m, pref):
    t = min(m, pref)
    assert m % t == 0, (m, t)
    return t


def _col_tile(n, pref, *offsets):
    t = (min(n, pref) // LANES) * LANES
    while t > LANES and (n % t or any(o % t for o in offsets)):
        t -= LANES
    assert t >= LANES and n % t == 0 and not any(o % t for o in offsets), (n, pref, offsets)
    return t


def _dot(a, b):
    return jnp.dot(a, b, preferred_element_type=F32)


def _resident(shape, index_map):
    return pl.BlockSpec(shape, index_map, pipeline_mode=pl.Buffered(1))


def _mm_kernel(x_ref, w_ref, o_ref):
    o_ref[...] = _dot(x_ref[...], w_ref[...].astype(BF16)).astype(o_ref.dtype)


def _wspec(k, tn, layer, col_block0=0):
    return pl.BlockSpec((None, k, tn), lambda i, j: (layer, 0, col_block0 + j))


def _matmul(x, w, layer, col0, ncols, out_dtype, name):
    m, k = x.shape
    tm = _row_tile(m, 2048)
    tn = _col_tile(ncols, 512, col0)
    return pl.pallas_call(
        _mm_kernel,
        out_shape=jax.ShapeDtypeStruct((m, ncols), out_dtype),
        grid=(m // tm, ncols // tn),
        in_specs=[_resident((tm, k), lambda i, j: (i, 0)), _wspec(k, tn, layer, col0 // tn)],
        out_specs=pl.BlockSpec((tm, tn), lambda i, j: (i, j)),
        compiler_params=_params("parallel", "parallel"),
        name=name,
    )(x, w)


def _mix_kernel(x_ref, oa_ref, og_ref, wga_ref, wgg_ref, wpa_ref, wpg_ref, o_ref):
    x = x_ref[...]
    gate_a = jax.nn.sigmoid(_dot(x, wga_ref[...]))
    gate_g = jax.nn.sigmoid(_dot(x, wgg_ref[...]))
    mix = gate_a * _dot(oa_ref[...], wpa_ref[...]) + gate_g * _dot(og_ref[...], wpg_ref[...])
    o_ref[...] = mix.astype(o_ref.dtype)


def _gated_mix(xb, oa, og, w_gate, w_pa, w_pg, layer):
    m, d = xb.shape
    da, dv = oa.shape[1], og.shape[1]
    tm = _row_tile(m, 1024)
    tn = _col_tile(d, 256)
    nj = d // tn
    return pl.pallas_call(
        _mix_kernel,
        out_shape=jax.ShapeDtypeStruct((m, d), BF16),
        grid=(m // tm, nj),
        in_specs=[pl.BlockSpec((tm, d), lambda i, j: (i, 0)),
                  pl.BlockSpec((tm, da), lambda i, j: (i, 0)),
                  pl.BlockSpec((tm, dv), lambda i, j: (i, 0)),
                  _wspec(d, tn, layer), _wspec(d, tn, layer, nj), _wspec(da, tn, layer), _wspec(dv, tn, layer)],
        out_specs=pl.BlockSpec((tm, tn), lambda i, j: (i, j)),
        compiler_params=_params("parallel", "parallel"),
        name="gated_mix",
    )(xb, oa, og, w_gate, w_gate, w_pa, w_pg)


def _mm_res_kernel(a_ref, w_ref, r_ref, o_ref, *, alpha):
    o_ref[...] = alpha * r_ref[...] + _dot(a_ref[...], w_ref[...].astype(BF16))


def _matmul_residual(a, w, layer, res, alpha):
    m, k = a.shape
    n = w.shape[2]
    tm = _row_tile(m, 2048)
    tn = _col_tile(n, 256)
    return pl.pallas_call(
        functools.partial(_mm_res_kernel, alpha=alpha),
        out_shape=jax.ShapeDtypeStruct((m, n), F32),
        grid=(m // tm, n // tn),
        in_specs=[_resident((tm, k), lambda i, j: (i, 0)), _wspec(k, tn, layer),
                  pl.BlockSpec((tm, tn), lambda i, j: (i, j))],
        out_specs=pl.BlockSpec((tm, tn), lambda i, j: (i, j)),
        compiler_params=_params("parallel", "parallel"),
        name="out_proj_residual",
    )(a, w, res)


def _ffn_out_kernel(act_ref, wd_ref, xb_ref, wg_ref, p_ref, wp_ref, r_ref, o_ref, *, alpha):
    f = _dot(act_ref[...], wd_ref[...])
    ple = jax.nn.sigmoid(_dot(xb_ref[...], wg_ref[...])) * _dot(p_ref[...], wp_ref[...])
    o_ref[...] = (alpha * r_ref[...] + f) + ple


def _ffn_out(act, w_down, xb, w_pgate, pb, w_pproj, layer, res, alpha):
    m, f = act.shape
    d = w_down.shape[2]
    pd = pb.shape[2]
    tm = _row_tile(m, 512)
    tn = _col_tile(d, 256)
    return pl.pallas_call(
        functools.partial(_ffn_out_kernel, alpha=alpha),
        out_shape=jax.ShapeDtypeStruct((m, d), F32),
        grid=(m // tm, d // tn),
        in_specs=[pl.BlockSpec((tm, f), lambda i, j: (i, 0)), _wspec(f, tn, layer),
                  pl.BlockSpec((tm, d), lambda i, j: (i, 0)), _wspec(d, tn, layer),
                  pl.BlockSpec((None, tm, pd), lambda i, j: (layer, i, 0)), _wspec(pd, tn, layer),
                  pl.BlockSpec((tm, tn), lambda i, j: (i, j))],
        out_specs=pl.BlockSpec((tm, tn), lambda i, j: (i, j)),
        compiler_params=_params("parallel", "parallel"),
        name="ffn_out",
    )(act, w_down, xb, w_pgate, pb, w_pproj, res)


def _ln_kernel(s_ref, g_ref, b_ref, o_ref, ob_ref):
    x = s_ref[...]
    mu = jnp.mean(x, axis=-1, keepdims=True)
    xc = x - mu
    var = jnp.mean(xc * xc, axis=-1, keepdims=True)
    y = xc * lax.rsqrt(var + LN_EPS) * g_ref[...] + b_ref[...]
    o_ref[...] = y
    ob_ref[...] = y.astype(ob_ref.dtype)


def _layer_norm(s, g, b):
    m, d = s.shape
    tm = _row_tile(m, 256)
    return pl.pallas_call(
        _ln_kernel,
        out_shape=(jax.ShapeDtypeStruct((m, d), F32), jax.ShapeDtypeStruct((m, d), BF16)),
        grid=(m // tm,),
        in_specs=[pl.BlockSpec((tm, d), lambda i: (i, 0)),
                  pl.BlockSpec((1, d), lambda i: (0, 0)),
                  pl.BlockSpec((1, d), lambda i: (0, 0))],
        out_specs=(pl.BlockSpec((tm, d), lambda i: (i, 0)), pl.BlockSpec((tm, d), lambda i: (i, 0))),
        compiler_params=_params("parallel"),
        name="layer_norm",
    )(s, g.reshape(1, d), b.reshape(1, d))


def _causal_conv(u, buf, w, cb):
    row = lax.broadcasted_iota(jnp.int32, u.shape, 0)
    u1 = jnp.where(row == 0, buf[1:2], pltpu.roll(u, 1, 0))
    u2 = jnp.where(row == 0, buf[0:1], jnp.where(row == 1, buf[1:2], pltpu.roll(u, 2, 0)))
    return cb + ((w[0:1] * u2 + w[1:2] * u1) + w[2:3] * u)


def _gelu_gate(c_gate, c_val):
    return (0.5 * c_gate * (1.0 + lax.erf(c_gate * (2.0 ** -0.5)))) * c_val


def _conv_gate_kernel(ug_ref, uv_ref, bg_ref, bv_ref, wg_ref, wv_ref, cg_ref, cv_ref, o_ref):
    c_gate = _causal_conv(ug_ref[...], bg_ref[...], wg_ref[...], cg_ref[...])
    c_val = _causal_conv(uv_ref[...], bv_ref[...], wv_ref[...], cv_ref[...])
    o_ref[...] = _gelu_gate(c_gate, c_val).astype(o_ref.dtype)


def _up_conv_kernel(x_ref, wg_ref, wv_ref, bg_ref, bv_ref, cwg_ref, cwv_ref, cg_ref, cv_ref,
                    o_ref, tg_ref, tv_ref):
    t = x_ref.shape[0]
    rc = UP_CONV_ROW_CHUNK if t % UP_CONV_ROW_CHUNK == 0 else t
    prev_g, prev_v = bg_ref[...], bv_ref[...]
    for r in range(t // rc):
        rows = slice(r * rc, (r + 1) * rc)
        u_gate = _dot(x_ref[rows, :], wg_ref[...].astype(BF16))
        u_val = _dot(x_ref[rows, :], wv_ref[...].astype(BF16))
        c_gate = _causal_conv(u_gate, prev_g, cwg_ref[...], cg_ref[...])
        c_val = _causal_conv(u_val, prev_v, cwv_ref[...], cv_ref[...])
        o_ref[rows, :] = _gelu_gate(c_gate, c_val).astype(o_ref.dtype)
        prev_g, prev_v = u_gate[rc - (CONV_W - 1):, :], u_val[rc - (CONV_W - 1):, :]
    tg_ref[...] = prev_g
    tv_ref[...] = prev_v


def _up_conv_gate(xb, w_up, layer, buf, conv_w, conv_b, bsz, t):
    d = xb.shape[1]
    f2 = w_up.shape[2]
    f = f2 // 2
    tc = _col_tile(f, 256)
    nj = f // tc
    cb = conv_b.reshape(1, f2)
    w_spec = lambda off: pl.BlockSpec((None, d, tc), lambda b, j: (layer, 0, off + j))
    b_spec = lambda off: pl.BlockSpec((None, CONV_W - 1, tc), lambda b, j: (b, 0, off + j))
    cw_spec = lambda off: pl.BlockSpec((CONV_W, tc), lambda b, j: (0, off + j))
    cb_spec = lambda off: pl.BlockSpec((1, tc), lambda b, j: (0, off + j))
    tail_spec = pl.BlockSpec((None, CONV_W - 1, tc), lambda b, j: (b, 0, j))
    act, tail_g, tail_v = pl.pallas_call(
        _up_conv_kernel,
        out_shape=(jax.ShapeDtypeStruct((bsz * t, f), BF16),
                   jax.ShapeDtypeStruct((bsz, CONV_W - 1, f), F32),
                   jax.ShapeDtypeStruct((bsz, CONV_W - 1, f), F32)),
        grid=(bsz, nj),
        in_specs=[_resident((t, d), lambda b, j: (b, 0)), w_spec(0), w_spec(nj), b_spec(0), b_spec(nj),
                  cw_spec(0), cw_spec(nj), cb_spec(0), cb_spec(nj)],
        out_specs=(pl.BlockSpec((t, tc), lambda b, j: (b, j)), tail_spec, tail_spec),
        compiler_params=_params("parallel", "parallel"),
        name="up_conv_gate",
    )(xb, w_up, w_up, buf, buf, conv_w, conv_w, cb, cb)
    return act, jnp.concatenate([tail_g, tail_v], axis=-1)


def _conv_gate(u3, buf, conv_w, conv_b):
    bsz, t, f2 = u3.shape
    f = f2 // 2
    tc = _col_tile(f, max(256, CONV_BLOCK_ELEMS // t))
    nj = f // tc
    cb = conv_b.reshape(1, f2)
    u_spec = lambda off: pl.BlockSpec((None, t, tc), lambda b, j: (b, 0, off + j))
    b_spec = lambda off: pl.BlockSpec((None, CONV_W - 1, tc), lambda b, j: (b, 0, off + j))
    w_spec = lambda off: pl.BlockSpec((CONV_W, tc), lambda b, j: (0, off + j))
    c_spec = lambda off: pl.BlockSpec((1, tc), lambda b, j: (0, off + j))
    return pl.pallas_call(
        _conv_gate_kernel,
        out_shape=jax.ShapeDtypeStruct((bsz, t, f), BF16),
        grid=(bsz, nj),
        in_specs=[u_spec(0), u_spec(nj), b_spec(0), b_spec(nj), w_spec(0), w_spec(nj), c_spec(0), c_spec(nj)],
        out_specs=pl.BlockSpec((None, t, tc), lambda b, j: (b, 0, j)),
        compiler_params=_params("parallel", "parallel"),
        name="conv_gate",
    )(u3, u3, buf, buf, conv_w, conv_w, cb, cb)


def _block_mean_kernel(k_ref, o_ref):
    o_ref[...] = jnp.sum(k_ref[...], axis=0, keepdims=True) * (1.0 / MOBA_BLOCK)


def _block_mean(k2d):
    m, da = k2d.shape
    nb = m // MOBA_BLOCK
    out = pl.pallas_call(
        _block_mean_kernel,
        out_shape=jax.ShapeDtypeStruct((nb, 1, da), F32),
        grid=(nb,),
        in_specs=[pl.BlockSpec((MOBA_BLOCK, da), lambda i: (i, 0))],
        out_specs=pl.BlockSpec((None, 1, da), lambda i: (i, 0, 0)),
        compiler_params=_params("parallel"),
        name="block_mean",
    )(k2d)
    return out.reshape(nb, da)


def _moba_prompt_kernel(q_ref, k_ref, v_ref, km_ref, o_ref, *, nb, scale):
    t = pl.program_id(2)
    blk = MOBA_BLOCK
    row = lax.broadcasted_iota(jnp.int32, (blk, blk), 0)
    col = lax.broadcasted_iota(jnp.int32, (blk, blk), 1)

    def attend(c):
        q = q_ref[...]
        off = c * blk
        s_own = lax.dot_general(q, k_ref[off:off + blk, :].astype(BF16), _NT, preferred_element_type=F32) * scale
        s_own = jnp.where(col <= row, s_own, NEG)
        m = jnp.max(s_own, axis=1, keepdims=True)
        if c > 0:
            s_past = lax.dot_general(q, k_ref[0:off, :].astype(BF16), _NT, preferred_element_type=F32) * scale
            if c > MOBA_TOPK:
                g = lax.dot_general(q, km_ref[...].astype(BF16), _NT, preferred_element_type=F32)
                lane = lax.broadcasted_iota(jnp.int32, g.shape, 1)
                g = jnp.where(lane < c, g, -jnp.inf)
                pieces = []
                for n in range(c):
                    gn = g[:, n:n + 1]
                    ahead = jnp.where(g > gn, 1.0, jnp.where((g == gn) & (lane < n), 1.0, 0.0))
                    rank = jnp.sum(ahead, axis=1, keepdims=True)
                    pieces.append(s_past[:, n * blk:(n + 1) * blk] + jnp.where(rank < MOBA_TOPK, 0.0, NEG))
                s_past = jnp.concatenate(pieces, axis=1)
            m = jnp.maximum(m, jnp.max(s_past, axis=1, keepdims=True))
        p_own = jnp.exp(s_own - m)
        denom = jnp.sum(p_own, axis=1, keepdims=True)
        o = _dot(p_own.astype(BF16), v_ref[off:off + blk, :].astype(BF16))
        if c > 0:
            p_past = jnp.exp(s_past - m)
            denom = denom + jnp.sum(p_past, axis=1, keepdims=True)
            o = o + _dot(p_past.astype(BF16), v_ref[0:off, :].astype(BF16))
        o_ref[...] = (o / denom).astype(o_ref.dtype)

    for c in range(nb):
        pl.when(t == c)(functools.partial(attend, c))


def _moba_prompt(qb, k2d, v2d, bsz, t):
    m, da = qb.shape
    assert t % MOBA_BLOCK == 0
    h = da // HD_A
    nb = t // MOBA_BLOCK
    kmean = _block_mean(k2d)
    return pl.pallas_call(
        functools.partial(_moba_prompt_kernel, nb=nb, scale=HD_A ** -0.5),
        out_shape=jax.ShapeDtypeStruct((m, da), BF16),
        grid=(bsz, h, nb),
        in_specs=[pl.BlockSpec((MOBA_BLOCK, HD_A), lambda b, hh, tt: (b * nb + tt, hh)),
                  pl.BlockSpec((t, HD_A), lambda b, hh, tt: (b, hh)),
                  pl.BlockSpec((t, HD_A), lambda b, hh, tt: (b, hh)),
                  pl.BlockSpec((nb, HD_A), lambda b, hh, tt: (b, hh))],
        out_specs=pl.BlockSpec((MOBA_BLOCK, HD_A), lambda b, hh, tt: (b * nb + tt, hh)),
        compiler_params=_params("parallel", "parallel", "parallel"),
        name="moba_prompt",
    )(qb, k2d, v2d, kmean)


def _page_mean_kernel(pt_ref, *refs):
    del pt_ref
    page_refs, o_ref = refs[:-1], refs[-1]
    nblk = o_ref.shape[0]
    ppb = len(page_refs) // nblk
    for n in range(nblk):
        total = jnp.sum(page_refs[n * ppb][...], axis=0)
        for r in page_refs[n * ppb + 1:(n + 1) * ppb]:
            total = total + jnp.sum(r[...], axis=0)
        o_ref[n] = total * (1.0 / MOBA_BLOCK)


def _page_block_mean(cache, page_table, layer):
    _, _, psz, h, hd = cache.shape
    bd, n_pages = page_table.shape
    ppb = MOBA_BLOCK // psz
    assert MOBA_BLOCK % psz == 0 and n_pages % ppb == 0
    nblk = n_pages // ppb
    bps = max(c for c in (4, 2, 1) if nblk % c == 0)
    pps = bps * ppb
    page_spec = lambda k: pl.BlockSpec(
        (None, None, psz, h, hd), lambda b, j, pt: (pt[b, pps * j + k], layer, 0, 0, 0))
    out = pl.pallas_call(
        _page_mean_kernel,
        out_shape=jax.ShapeDtypeStruct((bd, nblk, h, hd), F32),
        grid_spec=pltpu.PrefetchScalarGridSpec(
            num_scalar_prefetch=1,
            grid=(bd, nblk // bps),
            in_specs=[page_spec(k) for k in range(pps)],
            out_specs=pl.BlockSpec((None, bps, h, hd), lambda b, j, pt: (b, j, 0, 0))),
        compiler_params=_params("parallel", "parallel"),
        name="page_block_mean",
    )(page_table, *([cache] * pps))
    return out.reshape(bd, nblk, h * hd)


def _topk_kernel(q_ref, km_ref, o_ref, *, nblk):
    nheads, td = o_ref.shape[0], o_ref.shape[1]
    lane = lax.broadcasted_iota(jnp.int32, (td, nblk), 1).astype(F32)
    out_lane = lax.broadcasted_iota(jnp.int32, (td, LANES), 1)
    for hh in range(nheads):
        hs = slice(hh * HD_A, (hh + 1) * HD_A)
        s = lax.dot_general(q_ref[:, hs].astype(BF16), km_ref[:, hs].astype(BF16), _NT,
                            preferred_element_type=F32)
        out = jnp.zeros((td, LANES), jnp.int32)
        for r in range(MOBA_TOPK):
            mx = jnp.max(s, axis=1, keepdims=True)
            idx = jnp.min(jnp.where(s == mx, lane, float(nblk)), axis=1, keepdims=True)
            out = jnp.where(out_lane == r, idx.astype(jnp.int32), out)
            s = jnp.where(lane == idx, -jnp.inf, s)
        o_ref[hh] = out


def _moba_topk(q2d, kmean, bd, td):
    _, nblk, da = kmean.shape
    assert nblk >= MOBA_TOPK
    h = da // HD_A
    out = pl.pallas_call(
        functools.partial(_topk_kernel, nblk=nblk),
        out_shape=jax.ShapeDtypeStruct((bd, h, td, LANES), jnp.int32),
        grid=(bd,),
        in_specs=[pl.BlockSpec((td, da), lambda b: (b, 0)),
                  pl.BlockSpec((None, nblk, da), lambda b: (b, 0, 0))],
        out_specs=pl.BlockSpec((None, h, td, LANES), lambda b: (b, 0, 0, 0)),
        compiler_params=_params("parallel"),
        name="moba_topk",
    )(q2d, kmean)
    return out[..., :MOBA_TOPK]


def _moba_sample_kernel(pg_ref, q_ref, kn_ref, vn_ref, ck_hbm, cv_hbm, o_ref, kbuf, vbuf, sem,
                        *, layer, nheads, npg, scale):
    td = q_ref.shape[0]
    n = td * npg
    psz = kbuf.shape[2]
    step = pl.program_id(0) * nheads + pl.program_id(1)
    nsteps = pl.num_programs(0) * nheads
    slot = step % 2

    def page_copies(page, head, half, i):
        return (pltpu.make_async_copy(ck_hbm.at[page, layer, :, head, :], kbuf.at[half, i], sem.at[0, half]),
                pltpu.make_async_copy(cv_hbm.at[page, layer, :, head, :], vbuf.at[half, i], sem.at[1, half]))

    def start_step(s, half):
        head = s % nheads
        for i in range(n):
            for cp in page_copies(pg_ref[s * n + i], head, half, i):
                cp.start()

    @pl.when(step == 0)
    def _():
        start_step(step, slot)

    @pl.when(step + 1 < nsteps)
    def _():
        start_step(step + 1, 1 - slot)

    for i in range(n):
        for cp in page_copies(0, 0, slot, i):
            cp.wait()

    q = q_ref[...].astype(BF16)
    kc = kbuf[slot].reshape(n * psz, HD_A).astype(BF16)
    vc = vbuf[slot].reshape(n * psz, HD_A).astype(BF16)
    s_past = lax.dot_general(q, kc, _NT, preferred_element_type=F32) * scale
    prow = lax.broadcasted_iota(jnp.int32, s_past.shape, 0)
    pcol = lax.broadcasted_iota(jnp.int32, s_past.shape, 1)
    own = (pcol >= prow * (npg * psz)) & (pcol < (prow + 1) * (npg * psz))
    s_past = jnp.where(own, s_past, NEG)
    s_new = lax.dot_general(q, kn_ref[...].astype(BF16), _NT, preferred_element_type=F32) * scale
    row = lax.broadcasted_iota(jnp.int32, s_new.shape, 0)
    col = lax.broadcasted_iota(jnp.int32, s_new.shape, 1)
    s_new = jnp.where(col <= row, s_new, NEG)
    m = jnp.maximum(jnp.max(s_past, axis=1, keepdims=True), jnp.max(s_new, axis=1, keepdims=True))
    p_past = jnp.exp(s_past - m)
    p_new = jnp.exp(s_new - m)
    denom = jnp.sum(p_past, axis=1, keepdims=True) + jnp.sum(p_new, axis=1, keepdims=True)
    o_ref[...] = (_dot(p_past.astype(BF16), vc) + _dot(p_new.astype(BF16), vn_ref[...].astype(BF16))) / denom


def _moba_sample(q2d, kn2d, vn2d, cache_k, cache_v, page_table, layer, bd, td):
    _, _, psz, h, hd = cache_k.shape
    da = h * hd
    n_pages = page_table.shape[1]
    assert (n_pages * psz) % MOBA_BLOCK == 0 and td <= MOBA_BLOCK
    kmean = _page_block_mean(cache_k, page_table, layer)
    top = _moba_topk(q2d, kmean, bd, td)
    ppb = MOBA_BLOCK // psz
    npg = MOBA_TOPK * ppb
    logical = (top[..., None] * ppb + jnp.arange(ppb, dtype=jnp.int32)).reshape(bd, h * td * npg)
    pages = jnp.take_along_axis(page_table, logical, axis=1).reshape(-1)

    row_spec = pl.BlockSpec((td, HD_A), lambda b, hh, pg: (b, hh))
    hbm_spec = pl.BlockSpec(memory_space=pl.ANY)
    return pl.pallas_call(
        functools.partial(_moba_sample_kernel, layer=layer, nheads=h, npg=npg, scale=HD_A ** -0.5),
        out_shape=jax.ShapeDtypeStruct((bd * td, da), F32),
        grid_spec=pltpu.PrefetchScalarGridSpec(
            num_scalar_prefetch=1,
            grid=(bd, h),
            in_specs=[row_spec, row_spec, row_spec, hbm_spec, hbm_spec],
            out_specs=pl.BlockSpec((td, HD_A), lambda b, hh, pg: (b, hh)),
            scratch_shapes=[pltpu.VMEM((2, td * npg, psz, HD_A), F32),
                            pltpu.VMEM((2, td * npg, psz, HD_A), F32),
                            pltpu.SemaphoreType.DMA((2, 2))]),
        compiler_params=_params("arbitrary", "arbitrary"),
        name="moba_sample",
    )(pages, q2d, kn2d, vn2d, cache_k, cache_v)


def _gla_kernel(q_ref, k_ref, v_ref, r_ref, glr_ref, w2_ref, b2_ref, ng_ref, s0_ref,
                o_ref, sn_ref, state, *, nc, hg, dk, dv):
    c = pl.program_id(1)
    csz = q_ref.shape[0]

    @pl.when(c == 0)
    def _():
        state[...] = s0_ref[...]

    z = _dot(glr_ref[...].astype(BF16), w2_ref[...]) + b2_ref[...]
    g = (jnp.minimum(z, 0.0) - jnp.log1p(jnp.exp(-jnp.abs(z)))) * (1.0 / GLA_TAU)
    row = lax.broadcasted_iota(jnp.int32, (csz, csz), 0)
    col = lax.broadcasted_iota(jnp.int32, (csz, csz), 1)
    causal = col <= row
    cum = jnp.dot(causal.astype(F32), g, precision=lax.Precision.HIGHEST, preferred_element_type=F32)
    ones = jnp.ones((csz, LANES), F32)

    for hh in range(hg):
        ks = slice(hh * dk, (hh + 1) * dk)
        vs = slice(hh * dv, (hh + 1) * dv)
        cum_h = cum[:, ks]
        last = cum_h[csz - 1:csz, :]
        q = q_ref[:, ks] * (dk ** -0.5)
        k = k_ref[:, ks]
        vb = v_ref[:, vs].astype(BF16)
        s_prev = state[hh]

        qe = (q * jnp.exp(cum_h)).astype(BF16)
        ke = (k * jnp.exp(-cum_h)).astype(BF16)
        a = lax.dot_general(qe, ke, _NT, preferred_element_type=F32)
        a = jnp.where(causal, a, 0.0)
        o = _dot(qe, s_prev.astype(BF16)) + _dot(a.astype(BF16), vb)

        kd = (k * jnp.exp(last - cum_h)).astype(BF16)
        tot = lax.dot_general(g[:, ks], ones, _TN, precision=lax.Precision.HIGHEST,
                              preferred_element_type=F32)
        state[hh] = jnp.exp(tot[:, :1]) * s_prev + lax.dot_general(kd, vb, _TN, preferred_element_type=F32)

        mu = jnp.mean(o, axis=-1, keepdims=True)
        oc = o - mu
        var = jnp.mean(oc * oc, axis=-1, keepdims=True)
        on = oc * lax.rsqrt(var + LN_EPS) * ng_ref[...]
        r = r_ref[:, vs]
        o_ref[:, vs] = (on * (r * jax.nn.sigmoid(r))).astype(o_ref.dtype)

    @pl.when(c == nc - 1)
    def _():
        sn_ref[...] = state[...]


def _gla(qg, kg, vg, rg, glr_pad, w2_pad, layer, b2, norm_g, s0, s0_layer, bsz, t, out_dtype):
    m, dkt = qg.shape
    dvt = vg.shape[1]
    hg = s0.shape[2]
    dk, dv = dkt // hg, dvt // hg
    csz = GLA_CHUNK if t % GLA_CHUNK == 0 else t
    nc = t // csz
    rows = lambda w: pl.BlockSpec((csz, w), lambda b, c: (b * nc + c, 0))
    const = lambda shape: pl.BlockSpec(shape, lambda b, c: (0,) * len(shape))
    return pl.pallas_call(
        functools.partial(_gla_kernel, nc=nc, hg=hg, dk=dk, dv=dv),
        out_shape=(jax.ShapeDtypeStruct((m, dvt), out_dtype), jax.ShapeDtypeStruct((bsz, hg, dk, dv), F32)),
        grid=(bsz, nc),
        in_specs=[rows(dkt), rows(dkt), rows(dvt), rows(dvt), rows(LANES),
                  pl.BlockSpec((None, LANES, dkt), lambda b, c: (layer, 0, 0)), const((1, dkt)), const((1, dv)),
                  pl.BlockSpec((None, None, hg, dk, dv), lambda b, c: (b, s0_layer, 0, 0, 0))],
        out_specs=(rows(dvt), pl.BlockSpec((None, hg, dk, dv), lambda b, c: (b, 0, 0, 0))),
        scratch_shapes=[pltpu.VMEM((hg, dk, dv), F32)],
        compiler_params=_params("parallel", "arbitrary"),
        name="gla",
    )(qg, kg, vg, rg, glr_pad, w2_pad, b2.reshape(1, dkt), norm_g.reshape(1, dv), s0)


def _decoder_layer(x, xb, pb, bsz, t, w, i, moba_fn, s0, s0_layer, conv_buf, alpha, attn_dtype):
    d = x.shape[1]
    da, dkt, dvt = d // 2, d // 4, d // 2
    qa = _matmul(xb, w["w_in"], i, 0, da, attn_dtype, "in_proj_qa")
    ka = _matmul(xb, w["w_in"], i, da, da, F32, "in_proj_ka")
    va = _matmul(xb, w["w_in"], i, 2 * da, da, F32, "in_proj_va")
    off = 3 * da
    qg = _matmul(xb, w["w_in"], i, off, dkt, F32, "in_proj_qg")
    kg = _matmul(xb, w["w_in"], i, off + dkt, dkt, F32, "in_proj_kg")
    vg = _matmul(xb, w["w_in"], i, off + 2 * dkt, dvt, F32, "in_proj_vg")
    rg = _matmul(xb, w["w_in"], i, off + 2 * dkt + dvt, dvt, F32, "in_proj_rg")
    glr = _matmul(xb, w["w_glr"], i, 0, LANES, F32, "in_proj_glr")

    o_a = moba_fn(qa, ka, va).astype(BF16)
    o_g, s_new = _gla(qg, kg, vg, rg, glr, w["w_gla_a2"], i, w["b_gla_a"][i], w["gla_norm_g"][i],
                      s0, s0_layer, bsz, t, attn_dtype)
    o_g = o_g.astype(BF16)

    mix = _gated_mix(xb, o_a, o_g, w["w_merge_gate"], w["w_proj_a"], w["w_proj_g"], i)
    x1, x1b = _layer_norm(_matmul_residual(mix, w["w_out"], i, x, alpha), w["ln1_g"][i], w["ln1_b"][i])

    if t % BF16_SUBLANES == 0:
        act, conv_new = _up_conv_gate(x1b, w["w_up"], i, conv_buf, w["conv_w"][i], w["conv_b"][i], bsz, t)
    else:
        u3 = _matmul(x1b, w["w_up"], i, 0, w["w_up"].shape[2], F32, "up_proj").reshape(bsz, t, -1)
        act = _conv_gate(u3, conv_buf, w["conv_w"][i], w["conv_b"][i]).reshape(bsz * t, -1)
        conv_new = u3[:, t - (CONV_W - 1):]
    s2 = _ffn_out(act, w["w_down"], x1b, w["ple_gate"], pb, w["ple_proj"], i, x1, alpha)
    x2, x2b = _layer_norm(s2, w["ln2_g"][i], w["ln2_b"][i])
    return x2, x2b, ka, va, s_new, conv_new


def kernel(x_prompt, x_sample, cache_k, cache_v, state_gla, state_conv, page_table, p_prompt, p_sample,
           ln1_g, ln1_b, ln2_g, ln2_b, w_in, w_gla_a2, b_gla_a, gla_norm_g, w_proj_a, w_proj_g,
           w_merge_gate, w_out, w_up, conv_w, conv_b, w_down, ple_proj, ple_gate):
    bp, tp, d = x_prompt.shape
    bd, td, _ = x_sample.shape
    depth = w_in.shape[0]
    _, _, psz, h_a, hd = cache_k.shape
    assert hd == HD_A and psz == PAGE_SIZE
    hg, dk, dv = state_gla.shape[2:]
    f2 = w_up.shape[2]
    alpha = (2 * depth) ** 0.25
    n_main = w_in.shape[2] - GLA_RANK

    s0_prompt = jnp.zeros((bp, 1, hg, dk, dv), F32)
    conv0_prompt = jnp.zeros((bp, CONV_W - 1, f2), F32)

    w = {
        "w_in": w_in,
        "w_glr": jnp.pad(w_in[:, :, n_main:], ((0, 0), (0, 0), (0, LANES - GLA_RANK))).astype(BF16),
        "w_gla_a2": jnp.pad(w_gla_a2, ((0, 0), (0, LANES - GLA_RANK), (0, 0))).astype(BF16),
        "b_gla_a": b_gla_a, "gla_norm_g": gla_norm_g,
        "w_proj_a": w_proj_a.astype(BF16), "w_proj_g": w_proj_g.astype(BF16),
        "w_merge_gate": w_merge_gate.astype(BF16), "w_out": w_out,
        "w_up": w_up, "conv_w": conv_w, "conv_b": conv_b,
        "w_down": w_down.astype(BF16), "ple_proj": ple_proj.astype(BF16), "ple_gate": ple_gate.astype(BF16),
        "ln1_g": ln1_g, "ln1_b": ln1_b, "ln2_g": ln2_g, "ln2_b": ln2_b,
    }
    pb_prompt = p_prompt.reshape(depth, bp * tp, -1).astype(BF16)
    pb_sample = p_sample.reshape(depth, bd * td, -1).astype(BF16)

    xp, xpb = x_prompt.reshape(bp * tp, d), x_prompt.reshape(bp * tp, d).astype(BF16)
    xs, xsb = x_sample.reshape(bd * td, d), x_sample.reshape(bd * td, d).astype(BF16)
    outs = [[] for _ in range(8)]
    for i in range(depth):
        moba_p = lambda q, k, v: _moba_prompt(q, k, v, bp, tp)
        xp, xpb, kp, vp, sp, cp = _decoder_layer(
            xp, xpb, pb_prompt, bp, tp, w, i, moba_p, s0_prompt, 0, conv0_prompt, alpha, BF16)
        moba_s = lambda q, k, v: _moba_sample(q, k, v, cache_k, cache_v, page_table, i, bd, td)
        xs, xsb, kn, vn, sn, cn = _decoder_layer(
            xs, xsb, pb_sample, bd, td, w, i, moba_s, state_gla, i, state_conv[:, i], alpha, F32)
        for lst, val in zip(outs, (kp.reshape(bp, tp, h_a, hd), vp.reshape(bp, tp, h_a, hd), sp, cp,
                                   kn.reshape(bd, td, h_a, hd), vn.reshape(bd, td, h_a, hd), sn, cn)):
            lst.append(val)

    stacked = [jnp.stack(lst, axis=1) for lst in outs]
    return (xp.reshape(bp, tp, d), xs.reshape(bd, td, d), *stacked)
```

```python
import functools

import jax
import jax.numpy as jnp
from jax import lax
from jax.experimental import pallas as pl
from jax.experimental.pallas import tpu as pltpu

HD_A = 128
MOBA_BLOCK = 256
MOBA_TOPK = 3
PAGE_SIZE = 128
H_G = 4
GLA_RANK = 16
GLA_TAU = 16.0
GLA_CHUNK = 64
CONV_W = 3
LN_EPS = 1e-5

LANES = 128
BF16_SUBLANES = 16
V7X_VMEM_LIMIT_BYTES = 56 * 1024 * 1024
CONV_BLOCK_ELEMS = 512 * 1024
UP_CONV_ROW_CHUNK = 512

NEG = -1e30
BF16 = jnp.bfloat16
F32 = jnp.float32

_NT = (((1,), (1,)), ((), ()))
_TN = (((0,), (0,)), ((), ()))


def _params(*sem):
    return pltpu.CompilerParams(dimension_semantics=sem, vmem_limit_bytes=V7X_VMEM_LIMIT_BYTES)


def _row_tile(m, pref):
    t = min(m, pref)
    assert m % t == 0, (m, t)
    return t


def _col_tile(n, pref, *offsets):
    t = (min(n, pref) // LANES) * LANES
    while t > LANES and (n % t or any(o % t for o in offsets)):
        t -= LANES
    assert t >= LANES and n % t == 0 and not any(o % t for o in offsets), (n, pref, offsets)
    return t


def _dot(a, b):
    return jnp.dot(a, b, preferred_element_type=F32)


def _resident(shape, index_map):
    return pl.BlockSpec(shape, index_map, pipeline_mode=pl.Buffered(1))


def _mm_kernel(x_ref, w_ref, o_ref):
    o_ref[...] = _dot(x_ref[...], w_ref[...].astype(BF16)).astype(o_ref.dtype)


def _wspec(k, tn, layer, col_block0=0):
    return pl.BlockSpec((None, k, tn), lambda i, j: (layer, 0, col_block0 + j))


def _matmul(x, w, layer, col0, ncols, out_dtype, name):
    m, k = x.shape
    tm = _row_tile(m, 2048)
    tn = _col_tile(ncols, 512, col0)
    return pl.pallas_call(
        _mm_kernel,
        out_shape=jax.ShapeDtypeStruct((m, ncols), out_dtype),
        grid=(m // tm, ncols // tn),
        in_specs=[_resident((tm, k), lambda i, j: (i, 0)), _wspec(k, tn, layer, col0 // tn)],
        out_specs=pl.BlockSpec((tm, tn), lambda i, j: (i, j)),
        compiler_params=_params("parallel", "parallel"),
        name=name,
    )(x, w)


def _mix_kernel(x_ref, oa_ref, og_ref, wga_ref, wgg_ref, wpa_ref, wpg_ref, o_ref):
    x = x_ref[...]
    gate_a = jax.nn.sigmoid(_dot(x, wga_ref[...]))
    gate_g = jax.nn.sigmoid(_dot(x, wgg_ref[...]))
    mix = gate_a * _dot(oa_ref[...], wpa_ref[...]) + gate_g * _dot(og_ref[...], wpg_ref[...])
    o_ref[...] = mix.astype(o_ref.dtype)


def _gated_mix(xb, oa, og, w_gate, w_pa, w_pg, layer):
    m, d = xb.shape
    da, dv = oa.shape[1], og.shape[1]
    tm = _row_tile(m, 1024)
    tn = _col_tile(d, 256)
    nj = d // tn
    return pl.pallas_call(
        _mix_kernel,
        out_shape=jax.ShapeDtypeStruct((m, d), BF16),
        grid=(m // tm, nj),
        in_specs=[pl.BlockSpec((tm, d), lambda i, j: (i, 0)),
                  pl.BlockSpec((tm, da), lambda i, j: (i, 0)),
                  pl.BlockSpec((tm, dv), lambda i, j: (i, 0)),
                  _wspec(d, tn, layer), _wspec(d, tn, layer, nj), _wspec(da, tn, layer), _wspec(dv, tn, layer)],
        out_specs=pl.BlockSpec((tm, tn), lambda i, j: (i, j)),
        compiler_params=_params("parallel", "parallel"),
        name="gated_mix",
    )(xb, oa, og, w_gate, w_gate, w_pa, w_pg)


def _mm_nt_kernel(x_ref, wt_ref, o_ref):
    o_ref[...] = lax.dot_general(x_ref[...], wt_ref[...].astype(BF16), _NT,
                                 preferred_element_type=F32).astype(o_ref.dtype)


def _matmul_nt(x, wt, layer, ncols, out_dtype, name):
    m, k = x.shape
    tm = _row_tile(m, 2048)
    tn = _col_tile(ncols, 512)
    return pl.pallas_call(
        _mm_nt_kernel,
        out_shape=jax.ShapeDtypeStruct((m, ncols), out_dtype),
        grid=(m // tm, ncols // tn),
        in_specs=[_resident((tm, k), lambda i, j: (i, 0)),
                  pl.BlockSpec((None, tn, k), lambda i, j: (layer, j, 0))],
        out_specs=pl.BlockSpec((tm, tn), lambda i, j: (i, j)),
        compiler_params=_params("parallel", "parallel"),
        name=name,
    )(x, wt)


def _mm_res_kernel(a_ref, w_ref, r_ref, o_ref, *, alpha):
    o_ref[...] = alpha * r_ref[...] + _dot(a_ref[...], w_ref[...].astype(BF16))


def _matmul_residual(a, w, layer, res, alpha):
    m, k = a.shape
    n = w.shape[2]
    tm = _row_tile(m, 2048)
    tn = _col_tile(n, 256)
    return pl.pallas_call(
        functools.partial(_mm_res_kernel, alpha=alpha),
        out_shape=jax.ShapeDtypeStruct((m, n), F32),
        grid=(m // tm, n // tn),
        in_specs=[_resident((tm, k), lambda i, j: (i, 0)), _wspec(k, tn, layer),
                  pl.BlockSpec((tm, tn), lambda i, j: (i, j))],
        out_specs=pl.BlockSpec((tm, tn), lambda i, j: (i, j)),
        compiler_params=_params("parallel", "parallel"),
        name="out_proj_residual",
    )(a, w, res)


def _ffn_out_kernel(act_ref, wd_ref, xb_ref, wg_ref, p_ref, wp_ref, r_ref, o_ref, *, alpha):
    f = _dot(act_ref[...], wd_ref[...])
    ple = jax.nn.sigmoid(_dot(xb_ref[...], wg_ref[...])) * _dot(p_ref[...], wp_ref[...])
    o_ref[...] = (alpha * r_ref[...] + f) + ple


def _ffn_out(act, w_down, xb, w_pgate, pb, w_pproj, layer, res, alpha):
    m, f = act.shape
    d = w_down.shape[2]
    pd = pb.shape[2]
    tm = _row_tile(m, 512)
    tn = _col_tile(d, 256)
    return pl.pallas_call(
        functools.partial(_ffn_out_kernel, alpha=alpha),
        out_shape=jax.ShapeDtypeStruct((m, d), F32),
        grid=(m // tm, d // tn),
        in_specs=[pl.BlockSpec((tm, f), lambda i, j: (i, 0)), _wspec(f, tn, layer),
                  pl.BlockSpec((tm, d), lambda i, j: (i, 0)), _wspec(d, tn, layer),
                  pl.BlockSpec((None, tm, pd), lambda i, j: (layer, i, 0)), _wspec(pd, tn, layer),
                  pl.BlockSpec((tm, tn), lambda i, j: (i, j))],
        out_specs=pl.BlockSpec((tm, tn), lambda i, j: (i, j)),
        compiler_params=_params("parallel", "parallel"),
        name="ffn_out",
    )(act, w_down, xb, w_pgate, pb, w_pproj, res)


def _ln_kernel(s_ref, g_ref, b_ref, o_ref, ob_ref):
    x = s_ref[...]
    mu = jnp.mean(x, axis=-1, keepdims=True)
    xc = x - mu
    var = jnp.mean(xc * xc, axis=-1, keepdims=True)
    y = xc * lax.rsqrt(var + LN_EPS) * g_ref[...] + b_ref[...]
    o_ref[...] = y
    ob_ref[...] = y.astype(ob_ref.dtype)


def _layer_norm(s, g, b):
    m, d = s.shape
    tm = _row_tile(m, 256)
    return pl.pallas_call(
        _ln_kernel,
        out_shape=(jax.ShapeDtypeStruct((m, d), F32), jax.ShapeDtypeStruct((m, d), BF16)),
        grid=(m // tm,),
        in_specs=[pl.BlockSpec((tm, d), lambda i: (i, 0)),
                  pl.BlockSpec((1, d), lambda i: (0, 0)),
                  pl.BlockSpec((1, d), lambda i: (0, 0))],
        out_specs=(pl.BlockSpec((tm, d), lambda i: (i, 0)), pl.BlockSpec((tm, d), lambda i: (i, 0))),
        compiler_params=_params("parallel"),
        name="layer_norm",
    )(s, g.reshape(1, d), b.reshape(1, d))


def _causal_conv(u, buf, w, cb):
    row = lax.broadcasted_iota(jnp.int32, u.shape, 0)
    u1 = jnp.where(row == 0, buf[1:2], pltpu.roll(u, 1, 0))
    u2 = jnp.where(row == 0, buf[0:1], jnp.where(row == 1, buf[1:2], pltpu.roll(u, 2, 0)))
    return cb + ((w[0:1] * u2 + w[1:2] * u1) + w[2:3] * u)


def _gelu_gate(c_gate, c_val):
    return (0.5 * c_gate * (1.0 + lax.erf(c_gate * (2.0 ** -0.5)))) * c_val


def _conv_gate_kernel(ug_ref, uv_ref, bg_ref, bv_ref, wg_ref, wv_ref, cg_ref, cv_ref, o_ref):
    c_gate = _causal_conv(ug_ref[...], bg_ref[...], wg_ref[...], cg_ref[...])
    c_val = _causal_conv(uv_ref[...], bv_ref[...], wv_ref[...], cv_ref[...])
    o_ref[...] = _gelu_gate(c_gate, c_val).astype(o_ref.dtype)


def _up_conv_kernel(x_ref, wg_ref, wv_ref, bg_ref, bv_ref, cwg_ref, cwv_ref, cg_ref, cv_ref,
                    o_ref, tg_ref, tv_ref):
    t = x_ref.shape[0]
    rc = UP_CONV_ROW_CHUNK if t % UP_CONV_ROW_CHUNK == 0 else t
    prev_g, prev_v = bg_ref[...], bv_ref[...]
    for r in range(t // rc):
        rows = slice(r * rc, (r + 1) * rc)
        u_gate = _dot(x_ref[rows, :], wg_ref[...].astype(BF16))
        u_val = _dot(x_ref[rows, :], wv_ref[...].astype(BF16))
        c_gate = _causal_conv(u_gate, prev_g, cwg_ref[...], cg_ref[...])
        c_val = _causal_conv(u_val, prev_v, cwv_ref[...], cv_ref[...])
        o_ref[rows, :] = _gelu_gate(c_gate, c_val).astype(o_ref.dtype)
        prev_g, prev_v = u_gate[rc - (CONV_W - 1):, :], u_val[rc - (CONV_W - 1):, :]
    tg_ref[...] = prev_g
    tv_ref[...] = prev_v


def _up_conv_gate(xb, w_up, layer, buf, conv_w, conv_b, bsz, t):
    d = xb.shape[1]
    f2 = w_up.shape[2]
    f = f2 // 2
    tc = _col_tile(f, 256)
    nj = f // tc
    cb = conv_b.reshape(1, f2)
    w_spec = lambda off: pl.BlockSpec((None, d, tc), lambda b, j: (layer, 0, off + j))
    b_spec = lambda off: pl.BlockSpec((None, CONV_W - 1, tc), lambda b, j: (b, 0, off + j))
    cw_spec = lambda off: pl.BlockSpec((CONV_W, tc), lambda b, j: (0, off + j))
    cb_spec = lambda off: pl.BlockSpec((1, tc), lambda b, j: (0, off + j))
    tail_spec = pl.BlockSpec((None, CONV_W - 1, tc), lambda b, j: (b, 0, j))
    act, tail_g, tail_v = pl.pallas_call(
        _up_conv_kernel,
        out_shape=(jax.ShapeDtypeStruct((bsz * t, f), BF16),
                   jax.ShapeDtypeStruct((bsz, CONV_W - 1, f), F32),
                   jax.ShapeDtypeStruct((bsz, CONV_W - 1, f), F32)),
        grid=(bsz, nj),
        in_specs=[_resident((t, d), lambda b, j: (b, 0)), w_spec(0), w_spec(nj), b_spec(0), b_spec(nj),
                  cw_spec(0), cw_spec(nj), cb_spec(0), cb_spec(nj)],
        out_specs=(pl.BlockSpec((t, tc), lambda b, j: (b, j)), tail_spec, tail_spec),
        compiler_params=_params("parallel", "parallel"),
        name="up_conv_gate",
    )(xb, w_up, w_up, buf, buf, conv_w, conv_w, cb, cb)
    return act, jnp.concatenate([tail_g, tail_v], axis=-1)


def _conv_gate(u3, buf, conv_w, conv_b):
    bsz, t, f2 = u3.shape
    f = f2 // 2
    tc = _col_tile(f, max(256, CONV_BLOCK_ELEMS // t))
    nj = f // tc
    cb = conv_b.reshape(1, f2)
    u_spec = lambda off: pl.BlockSpec((None, t, tc), lambda b, j: (b, 0, off + j))
    b_spec = lambda off: pl.BlockSpec((None, CONV_W - 1, tc), lambda b, j: (b, 0, off + j))
    w_spec = lambda off: pl.BlockSpec((CONV_W, tc), lambda b, j: (0, off + j))
    c_spec = lambda off: pl.BlockSpec((1, tc), lambda b, j: (0, off + j))
    return pl.pallas_call(
        _conv_gate_kernel,
        out_shape=jax.ShapeDtypeStruct((bsz, t, f), BF16),
        grid=(bsz, nj),
        in_specs=[u_spec(0), u_spec(nj), b_spec(0), b_spec(nj), w_spec(0), w_spec(nj), c_spec(0), c_spec(nj)],
        out_specs=pl.BlockSpec((None, t, tc), lambda b, j: (b, 0, j)),
        compiler_params=_params("parallel", "parallel"),
        name="conv_gate",
    )(u3, u3, buf, buf, conv_w, conv_w, cb, cb)


def _block_mean_kernel(k_ref, o_ref):
    o_ref[...] = jnp.sum(k_ref[...], axis=0, keepdims=True) * (1.0 / MOBA_BLOCK)


def _block_mean(hmat, da, col_block):
    nb = hmat.shape[0] // MOBA_BLOCK
    out = pl.pallas_call(
        _block_mean_kernel,
        out_shape=jax.ShapeDtypeStruct((nb, 1, da), F32),
        grid=(nb,),
        in_specs=[pl.BlockSpec((MOBA_BLOCK, da), lambda i: (i, col_block))],
        out_specs=pl.BlockSpec((None, 1, da), lambda i: (i, 0, 0)),
        compiler_params=_params("parallel"),
        name="block_mean",
    )(hmat)
    return out.reshape(nb, da)


def _moba_prompt_kernel(q_ref, k_ref, v_ref, km_ref, o_ref, *, nb, scale):
    t = pl.program_id(2)
    blk = MOBA_BLOCK
    row = lax.broadcasted_iota(jnp.int32, (blk, blk), 0)
    col = lax.broadcasted_iota(jnp.int32, (blk, blk), 1)

    def attend(c):
        q = q_ref[...].astype(BF16)
        off = c * blk
        s_own = lax.dot_general(q, k_ref[off:off + blk, :].astype(BF16), _NT, preferred_element_type=F32) * scale
        s_own = jnp.where(col <= row, s_own, NEG)
        m = jnp.max(s_own, axis=1, keepdims=True)
        if c > 0:
            s_past = lax.dot_general(q, k_ref[0:off, :].astype(BF16), _NT, preferred_element_type=F32) * scale
            if c > MOBA_TOPK:
                g = lax.dot_general(q, km_ref[...].astype(BF16), _NT, preferred_element_type=F32)
                lane = lax.broadcasted_iota(jnp.int32, g.shape, 1)
                g = jnp.where(lane < c, g, -jnp.inf)
                pieces = []
                for n in range(c):
                    gn = g[:, n:n + 1]
                    ahead = jnp.where(g > gn, 1.0, jnp.where((g == gn) & (lane < n), 1.0, 0.0))
                    rank = jnp.sum(ahead, axis=1, keepdims=True)
                    pieces.append(s_past[:, n * blk:(n + 1) * blk] + jnp.where(rank < MOBA_TOPK, 0.0, NEG))
                s_past = jnp.concatenate(pieces, axis=1)
            m = jnp.maximum(m, jnp.max(s_past, axis=1, keepdims=True))
        p_own = jnp.exp(s_own - m)
        denom = jnp.sum(p_own, axis=1, keepdims=True)
        o = _dot(p_own.astype(BF16), v_ref[off:off + blk, :].astype(BF16))
        if c > 0:
            p_past = jnp.exp(s_past - m)
            denom = denom + jnp.sum(p_past, axis=1, keepdims=True)
            o = o + _dot(p_past.astype(BF16), v_ref[0:off, :].astype(BF16))
        o_ref[...] = (o / denom).astype(o_ref.dtype)

    for c in range(nb):
        pl.when(t == c)(functools.partial(attend, c))


def _moba_prompt(hmat, da, bsz, t):
    m = hmat.shape[0]
    assert t % MOBA_BLOCK == 0
    h = da // HD_A
    nb = t // MOBA_BLOCK
    kmean = _block_mean(hmat, da, 1)
    return pl.pallas_call(
        functools.partial(_moba_prompt_kernel, nb=nb, scale=HD_A ** -0.5),
        out_shape=jax.ShapeDtypeStruct((m, da), BF16),
        grid=(bsz, h, nb),
        in_specs=[pl.BlockSpec((MOBA_BLOCK, HD_A), lambda b, hh, tt: (b * nb + tt, hh)),
                  pl.BlockSpec((t, HD_A), lambda b, hh, tt: (b, h + hh)),
                  pl.BlockSpec((t, HD_A), lambda b, hh, tt: (b, 2 * h + hh)),
                  pl.BlockSpec((nb, HD_A), lambda b, hh, tt: (b, hh))],
        out_specs=pl.BlockSpec((MOBA_BLOCK, HD_A), lambda b, hh, tt: (b * nb + tt, hh)),
        compiler_params=_params("parallel", "parallel", "parallel"),
        name="moba_prompt",
    )(hmat, hmat, hmat, kmean)


def _page_mean_kernel(pt_ref, *refs):
    del pt_ref
    page_refs, o_ref = refs[:-1], refs[-1]
    nblk = o_ref.shape[0]
    ppb = len(page_refs) // nblk
    for n in range(nblk):
        total = jnp.sum(page_refs[n * ppb][...], axis=0)
        for r in page_refs[n * ppb + 1:(n + 1) * ppb]:
            total = total + jnp.sum(r[...], axis=0)
        o_ref[n] = total * (1.0 / MOBA_BLOCK)


def _page_block_mean(cache, page_table, layer):
    _, _, psz, h, hd = cache.shape
    bd, n_pages = page_table.shape
    ppb = MOBA_BLOCK // psz
    assert MOBA_BLOCK % psz == 0 and n_pages % ppb == 0
    nblk = n_pages // ppb
    bps = max(c for c in (4, 2, 1) if nblk % c == 0)
    pps = bps * ppb
    page_spec = lambda k: pl.BlockSpec(
        (None, None, psz, h, hd), lambda b, j, pt: (pt[b, pps * j + k], layer, 0, 0, 0))
    out = pl.pallas_call(
        _page_mean_kernel,
        out_shape=jax.ShapeDtypeStruct((bd, nblk, h, hd), F32),
        grid_spec=pltpu.PrefetchScalarGridSpec(
            num_scalar_prefetch=1,
            grid=(bd, nblk // bps),
            in_specs=[page_spec(k) for k in range(pps)],
            out_specs=pl.BlockSpec((None, bps, h, hd), lambda b, j, pt: (b, j, 0, 0))),
        compiler_params=_params("parallel", "parallel"),
        name="page_block_mean",
    )(page_table, *([cache] * pps))
    return out.reshape(bd, nblk, h * hd)


def _topk_kernel(q_ref, km_ref, o_ref, *, nblk):
    nheads, td = o_ref.shape[0], o_ref.shape[1]
    lane = lax.broadcasted_iota(jnp.int32, (td, nblk), 1).astype(F32)
    out_lane = lax.broadcasted_iota(jnp.int32, (td, LANES), 1)
    for hh in range(nheads):
        hs = slice(hh * HD_A, (hh + 1) * HD_A)
        s = lax.dot_general(q_ref[:, hs].astype(BF16), km_ref[:, hs].astype(BF16), _NT,
                            preferred_element_type=F32)
        out = jnp.zeros((td, LANES), jnp.int32)
        for r in range(MOBA_TOPK):
            mx = jnp.max(s, axis=1, keepdims=True)
            idx = jnp.min(jnp.where(s == mx, lane, float(nblk)), axis=1, keepdims=True)
            out = jnp.where(out_lane == r, idx.astype(jnp.int32), out)
            s = jnp.where(lane == idx, -jnp.inf, s)
        o_ref[hh] = out


def _moba_topk(q2d, kmean, bd, td):
    _, nblk, da = kmean.shape
    assert nblk >= MOBA_TOPK
    h = da // HD_A
    out = pl.pallas_call(
        functools.partial(_topk_kernel, nblk=nblk),
        out_shape=jax.ShapeDtypeStruct((bd, h, td, LANES), jnp.int32),
        grid=(bd,),
        in_specs=[pl.BlockSpec((td, da), lambda b: (b, 0)),
                  pl.BlockSpec((None, nblk, da), lambda b: (b, 0, 0))],
        out_specs=pl.BlockSpec((None, h, td, LANES), lambda b: (b, 0, 0, 0)),
        compiler_params=_params("parallel"),
        name="moba_topk",
    )(q2d, kmean)
    return out[..., :MOBA_TOPK]


def _moba_sample_kernel(pg_ref, q_ref, kn_ref, vn_ref, ck_hbm, cv_hbm, o_ref, kbuf, vbuf, sem,
                        *, layer, nheads, npg, scale):
    td = q_ref.shape[0]
    n = td * npg
    psz = kbuf.shape[2]
    step = pl.program_id(0) * nheads + pl.program_id(1)
    nsteps = pl.num_programs(0) * nheads
    slot = step % 2

    def page_copies(page, head, half, i):
        return (pltpu.make_async_copy(ck_hbm.at[page, layer, :, head, :], kbuf.at[half, i], sem.at[0, half]),
                pltpu.make_async_copy(cv_hbm.at[page, layer, :, head, :], vbuf.at[half, i], sem.at[1, half]))

    def start_step(s, half):
        head = s % nheads
        for i in range(n):
            for cp in page_copies(pg_ref[s * n + i], head, half, i):
                cp.start()

    @pl.when(step == 0)
    def _():
        start_step(step, slot)

    @pl.when(step + 1 < nsteps)
    def _():
        start_step(step + 1, 1 - slot)

    for i in range(n):
        for cp in page_copies(0, 0, slot, i):
            cp.wait()

    q = q_ref[...].astype(BF16)
    kc = kbuf[slot].reshape(n * psz, HD_A).astype(BF16)
    vc = vbuf[slot].reshape(n * psz, HD_A).astype(BF16)
    s_past = lax.dot_general(q, kc, _NT, preferred_element_type=F32) * scale
    prow = lax.broadcasted_iota(jnp.int32, s_past.shape, 0)
    pcol = lax.broadcasted_iota(jnp.int32, s_past.shape, 1)
    own = (pcol >= prow * (npg * psz)) & (pcol < (prow + 1) * (npg * psz))
    s_past = jnp.where(own, s_past, NEG)
    s_new = lax.dot_general(q, kn_ref[...].astype(BF16), _NT, preferred_element_type=F32) * scale
    row = lax.broadcasted_iota(jnp.int32, s_new.shape, 0)
    col = lax.broadcasted_iota(jnp.int32, s_new.shape, 1)
    s_new = jnp.where(col <= row, s_new, NEG)
    m = jnp.maximum(jnp.max(s_past, axis=1, keepdims=True), jnp.max(s_new, axis=1, keepdims=True))
    p_past = jnp.exp(s_past - m)
    p_new = jnp.exp(s_new - m)
    denom = jnp.sum(p_past, axis=1, keepdims=True) + jnp.sum(p_new, axis=1, keepdims=True)
    o_ref[...] = (_dot(p_past.astype(BF16), vc) + _dot(p_new.astype(BF16), vn_ref[...].astype(BF16))) / denom


def _moba_sample(hmat, cache_k, cache_v, page_table, layer, bd, td):
    _, _, psz, h, hd = cache_k.shape
    da = h * hd
    n_pages = page_table.shape[1]
    assert (n_pages * psz) % MOBA_BLOCK == 0 and td <= MOBA_BLOCK
    kmean = _page_block_mean(cache_k, page_table, layer)
    top = _moba_topk(hmat, kmean, bd, td)
    ppb = MOBA_BLOCK // psz
    npg = MOBA_TOPK * ppb
    logical = (top[..., None] * ppb + jnp.arange(ppb, dtype=jnp.int32)).reshape(bd, h * td * npg)
    pages = jnp.take_along_axis(page_table, logical, axis=1).reshape(-1)

    row_spec = lambda group: pl.BlockSpec((td, HD_A), lambda b, hh, pg: (b, group * h + hh))
    hbm_spec = pl.BlockSpec(memory_space=pl.ANY)
    return pl.pallas_call(
        functools.partial(_moba_sample_kernel, layer=layer, nheads=h, npg=npg, scale=HD_A ** -0.5),
        out_shape=jax.ShapeDtypeStruct((bd * td, da), F32),
        grid_spec=pltpu.PrefetchScalarGridSpec(
            num_scalar_prefetch=1,
            grid=(bd, h),
            in_specs=[row_spec(0), row_spec(1), row_spec(2), hbm_spec, hbm_spec],
            out_specs=pl.BlockSpec((td, HD_A), lambda b, hh, pg: (b, hh)),
            scratch_shapes=[pltpu.VMEM((2, td * npg, psz, HD_A), F32),
                            pltpu.VMEM((2, td * npg, psz, HD_A), F32),
                            pltpu.SemaphoreType.DMA((2, 2))]),
        compiler_params=_params("arbitrary", "arbitrary"),
        name="moba_sample",
    )(pages, hmat, hmat, hmat, cache_k, cache_v)


def _gla_kernel(q_ref, k_ref, v_ref, r_ref, glr_ref, w2_ref, b2_ref, ng_ref, s0_ref,
                o_ref, sn_ref, state, *, nc, hg, dk, dv):
    c = pl.program_id(1)
    csz = q_ref.shape[0]

    @pl.when(c == 0)
    def _():
        state[...] = s0_ref[...]

    z = _dot(glr_ref[...].astype(BF16), w2_ref[...]) + b2_ref[...]
    g = (jnp.minimum(z, 0.0) - jnp.log1p(jnp.exp(-jnp.abs(z)))) * (1.0 / GLA_TAU)
    row = lax.broadcasted_iota(jnp.int32, (csz, csz), 0)
    col = lax.broadcasted_iota(jnp.int32, (csz, csz), 1)
    causal = col <= row
    cum = jnp.dot(causal.astype(F32), g, precision=lax.Precision.HIGHEST, preferred_element_type=F32)
    ones = jnp.ones((csz, LANES), F32)

    for hh in range(hg):
        ks = slice(hh * dk, (hh + 1) * dk)
        vs = slice(hh * dv, (hh + 1) * dv)
        cum_h = cum[:, ks]
        last = cum_h[csz - 1:csz, :]
        q = q_ref[:, ks] * (dk ** -0.5)
        k = k_ref[:, ks]
        vb = v_ref[:, vs].astype(BF16)
        s_prev = state[hh]

        qe = (q * jnp.exp(cum_h)).astype(BF16)
        ke = (k * jnp.exp(-cum_h)).astype(BF16)
        a = lax.dot_general(qe, ke, _NT, preferred_element_type=F32)
        a = jnp.where(causal, a, 0.0)
        o = _dot(qe, s_prev.astype(BF16)) + _dot(a.astype(BF16), vb)

        kd = (k * jnp.exp(last - cum_h)).astype(BF16)
        tot = lax.dot_general(g[:, ks], ones, _TN, precision=lax.Precision.HIGHEST,
                              preferred_element_type=F32)
        state[hh] = jnp.exp(tot[:, :1]) * s_prev + lax.dot_general(kd, vb, _TN, preferred_element_type=F32)

        mu = jnp.mean(o, axis=-1, keepdims=True)
        oc = o - mu
        var = jnp.mean(oc * oc, axis=-1, keepdims=True)
        on = oc * lax.rsqrt(var + LN_EPS) * ng_ref[...]
        r = r_ref[:, vs]
        o_ref[:, vs] = (on * (r * jax.nn.sigmoid(r))).astype(o_ref.dtype)

    @pl.when(c == nc - 1)
    def _():
        sn_ref[...] = state[...]


def _gla(hmat, col0, glr_pad, w2_pad, layer, b2, norm_g, s0, s0_layer, bsz, t, out_dtype):
    m = hmat.shape[0]
    hg, dk, dv = s0.shape[2:]
    dkt, dvt = hg * dk, hg * dv
    assert col0 % dkt == 0 and (col0 + 2 * dkt) % dvt == 0
    qcol, vcol = col0 // dkt, (col0 + 2 * dkt) // dvt
    csz = GLA_CHUNK if t % GLA_CHUNK == 0 else t
    nc = t // csz
    rows = lambda w, col=0: pl.BlockSpec((csz, w), lambda b, c: (b * nc + c, col))
    const = lambda shape: pl.BlockSpec(shape, lambda b, c: (0,) * len(shape))
    return pl.pallas_call(
        functools.partial(_gla_kernel, nc=nc, hg=hg, dk=dk, dv=dv),
        out_shape=(jax.ShapeDtypeStruct((m, dvt), out_dtype), jax.ShapeDtypeStruct((bsz, hg, dk, dv), F32)),
        grid=(bsz, nc),
        in_specs=[rows(dkt, qcol), rows(dkt, qcol + 1), rows(dvt, vcol), rows(dvt, vcol + 1), rows(LANES),
                  pl.BlockSpec((None, LANES, dkt), lambda b, c: (layer, 0, 0)), const((1, dkt)), const((1, dv)),
                  pl.BlockSpec((None, None, hg, dk, dv), lambda b, c: (b, s0_layer, 0, 0, 0))],
        out_specs=(rows(dvt), pl.BlockSpec((None, hg, dk, dv), lambda b, c: (b, 0, 0, 0))),
        scratch_shapes=[pltpu.VMEM((hg, dk, dv), F32)],
        compiler_params=_params("parallel", "arbitrary"),
        name="gla",
    )(hmat, hmat, hmat, hmat, glr_pad, w2_pad, b2.reshape(1, dkt), norm_g.reshape(1, dv), s0)


def _decoder_layer(x, xb, pb, bsz, t, w, i, moba_fn, s0, s0_layer, conv_buf, alpha, attn_dtype):
    d = x.shape[1]
    da = d // 2
    hmat = _matmul_nt(xb, w["w_in_t"], i, w["w_in_t"].shape[1] - GLA_RANK, F32, "in_proj")
    glr = _matmul_nt(xb, w["w_glr_t"], i, LANES, F32, "in_proj_glr")
    ka, va = hmat[:, da:2 * da], hmat[:, 2 * da:3 * da]

    o_a = moba_fn(hmat).astype(BF16)
    o_g, s_new = _gla(hmat, 3 * da, glr, w["w_gla_a2"], i, w["b_gla_a"][i], w["gla_norm_g"][i],
                      s0, s0_layer, bsz, t, attn_dtype)
    o_g = o_g.astype(BF16)

    mix = _gated_mix(xb, o_a, o_g, w["w_merge_gate"], w["w_proj_a"], w["w_proj_g"], i)
    x1, x1b = _layer_norm(_matmul_residual(mix, w["w_out"], i, x, alpha), w["ln1_g"][i], w["ln1_b"][i])

    if t % BF16_SUBLANES == 0:
        act, conv_new = _up_conv_gate(x1b, w["w_up"], i, conv_buf, w["conv_w"][i], w["conv_b"][i], bsz, t)
    else:
        u3 = _matmul(x1b, w["w_up"], i, 0, w["w_up"].shape[2], F32, "up_proj").reshape(bsz, t, -1)
        act = _conv_gate(u3, conv_buf, w["conv_w"][i], w["conv_b"][i]).reshape(bsz * t, -1)
        conv_new = u3[:, t - (CONV_W - 1):]
    s2 = _ffn_out(act, w["w_down"], x1b, w["ple_gate"], pb, w["ple_proj"], i, x1, alpha)
    x2, x2b = _layer_norm(s2, w["ln2_g"][i], w["ln2_b"][i])
    return x2, x2b, ka, va, s_new, conv_new


def kernel(x_prompt, x_sample, cache_k, cache_v, state_gla, state_conv, page_table, p_prompt, p_sample,
           ln1_g, ln1_b, ln2_g, ln2_b, w_in, w_gla_a2, b_gla_a, gla_norm_g, w_proj_a, w_proj_g,
           w_merge_gate, w_out, w_up, conv_w, conv_b, w_down, ple_proj, ple_gate):
    bp, tp, d = x_prompt.shape
    bd, td, _ = x_sample.shape
    depth = w_in.shape[0]
    _, _, psz, h_a, hd = cache_k.shape
    assert hd == HD_A and psz == PAGE_SIZE
    hg, dk, dv = state_gla.shape[2:]
    f2 = w_up.shape[2]
    alpha = (2 * depth) ** 0.25
    n_main = w_in.shape[2] - GLA_RANK

    s0_prompt = jnp.zeros((bp, 1, hg, dk, dv), F32)
    conv0_prompt = jnp.zeros((bp, CONV_W - 1, f2), F32)

    w_in_t = jnp.swapaxes(w_in, 1, 2)
    w = {
        "w_in_t": w_in_t,
        "w_glr_t": jnp.pad(w_in_t[:, n_main:, :], ((0, 0), (0, LANES - GLA_RANK), (0, 0))).astype(BF16),
        "w_gla_a2": jnp.pad(w_gla_a2, ((0, 0), (0, LANES - GLA_RANK), (0, 0))).astype(BF16),
        "b_gla_a": b_gla_a, "gla_norm_g": gla_norm_g,
        "w_proj_a": w_proj_a.astype(BF16), "w_proj_g": w_proj_g.astype(BF16),
        "w_merge_gate": w_merge_gate.astype(BF16), "w_out": w_out,
        "w_up": w_up, "conv_w": conv_w, "conv_b": conv_b,
        "w_down": w_down.astype(BF16), "ple_proj": ple_proj.astype(BF16), "ple_gate": ple_gate.astype(BF16),
        "ln1_g": ln1_g, "ln1_b": ln1_b, "ln2_g": ln2_g, "ln2_b": ln2_b,
    }
    pb_prompt = p_prompt.reshape(depth, bp * tp, -1).astype(BF16)
    pb_sample = p_sample.reshape(depth, bd * td, -1).astype(BF16)

    xp, xpb = x_prompt.reshape(bp * tp, d), x_prompt.reshape(bp * tp, d).astype(BF16)
    xs, xsb = x_sample.reshape(bd * td, d), x_sample.reshape(bd * td, d).astype(BF16)
    outs = [[] for _ in range(8)]
    for i in range(depth):
        moba_p = lambda hmat: _moba_prompt(hmat, h_a * hd, bp, tp)
        xp, xpb, kp, vp, sp, cp = _decoder_layer(
            xp, xpb, pb_prompt, bp, tp, w, i, moba_p, s0_prompt, 0, conv0_prompt, alpha, BF16)
        moba_s = lambda hmat: _moba_sample(hmat, cache_k, cache_v, page_table, i, bd, td)
        xs, xsb, kn, vn, sn, cn = _decoder_layer(
            xs, xsb, pb_sample, bd, td, w, i, moba_s, state_gla, i, state_conv[:, i], alpha, F32)
        for lst, val in zip(outs, (kp.reshape(bp, tp, h_a, hd), vp.reshape(bp, tp, h_a, hd), sp, cp,
                                   kn.reshape(bd, td, h_a, hd), vn.reshape(bd, td, h_a, hd), sn, cn)):
            lst.append(val)

    stacked = [jnp.stack(lst, axis=1) for lst in outs]
    return (xp.reshape(bp, tp, d), xs.reshape(bd, td, d), *stacked)
```

```python
import functools

import jax
import jax.numpy as jnp
from jax import lax
from jax.experimental import pallas as pl
from jax.experimental.pallas import tpu as pltpu

HD_A = 128
MOBA_BLOCK = 256
MOBA_TOPK = 3
PAGE_SIZE = 128
H_G = 4
GLA_RANK = 16
GLA_TAU = 16.0
GLA_CHUNK = 64
GLA_FACTORED_MIN_LOG_DECAY = -60.0
CONV_W = 3
LN_EPS = 1e-5

LANES = 128
BF16_SUBLANES = 16
V7X_VMEM_LIMIT_BYTES = 56 * 1024 * 1024
CONV_BLOCK_ELEMS = 512 * 1024
MOBA_HEADS_PER_STEP = 2
UP_CONV_ROW_CHUNK = 512

NEG = -1e30
BF16 = jnp.bfloat16
F32 = jnp.float32

_NT = (((1,), (1,)), ((), ()))
_TN = (((0,), (0,)), ((), ()))


def _params(*sem):
    return pltpu.CompilerParams(dimension_semantics=sem, vmem_limit_bytes=V7X_VMEM_LIMIT_BYTES)


def _row_tile(m, pref):
    t = min(m, pref)
    assert m % t == 0, (m, t)
    return t


def _col_tile(n, pref, *offsets):
    t = (min(n, pref) // LANES) * LANES
    while t > LANES and (n % t or any(o % t for o in offsets)):
        t -= LANES
    assert t >= LANES and n % t == 0 and not any(o % t for o in offsets), (n, pref, offsets)
    return t


def _dot(a, b):
    return jnp.dot(a, b, preferred_element_type=F32)


def _resident(shape, index_map):
    return pl.BlockSpec(shape, index_map, pipeline_mode=pl.Buffered(1))


def _mm_kernel(x_ref, w_ref, o_ref):
    o_ref[...] = _dot(x_ref[...], w_ref[...].astype(BF16)).astype(o_ref.dtype)


def _wspec(k, tn, layer, col_block0=0):
    return pl.BlockSpec((None, k, tn), lambda i, j: (layer, 0, col_block0 + j))


def _matmul(x, w, layer, col0, ncols, out_dtype, name):
    m, k = x.shape
    tm = _row_tile(m, 2048)
    tn = _col_tile(ncols, 512, col0)
    return pl.pallas_call(
        _mm_kernel,
        out_shape=jax.ShapeDtypeStruct((m, ncols), out_dtype),
        grid=(m // tm, ncols // tn),
        in_specs=[_resident((tm, k), lambda i, j: (i, 0)), _wspec(k, tn, layer, col0 // tn)],
        out_specs=pl.BlockSpec((tm, tn), lambda i, j: (i, j)),
        compiler_params=_params("parallel", "parallel"),
        name=name,
    )(x, w)


def _mix_kernel(x_ref, oa_ref, og_ref, wga_ref, wgg_ref, wpa_ref, wpg_ref, o_ref):
    x = x_ref[...]
    gate_a = jax.nn.sigmoid(_dot(x, wga_ref[...]))
    gate_g = jax.nn.sigmoid(_dot(x, wgg_ref[...]))
    mix = gate_a * _dot(oa_ref[...], wpa_ref[...]) + gate_g * _dot(og_ref[...], wpg_ref[...])
    o_ref[...] = mix.astype(o_ref.dtype)


def _gated_mix(xb, oa, og, w_gate, w_pa, w_pg, layer):
    m, d = xb.shape
    da, dv = oa.shape[1], og.shape[1]
    tm = _row_tile(m, 1024)
    tn = _col_tile(d, 256)
    nj = d // tn
    return pl.pallas_call(
        _mix_kernel,
        out_shape=jax.ShapeDtypeStruct((m, d), BF16),
        grid=(m // tm, nj),
        in_specs=[pl.BlockSpec((tm, d), lambda i, j: (i, 0)),
                  pl.BlockSpec((tm, da), lambda i, j: (i, 0)),
                  pl.BlockSpec((tm, dv), lambda i, j: (i, 0)),
                  _wspec(d, tn, layer), _wspec(d, tn, layer, nj), _wspec(da, tn, layer), _wspec(dv, tn, layer)],
        out_specs=pl.BlockSpec((tm, tn), lambda i, j: (i, j)),
        compiler_params=_params("parallel", "parallel"),
        name="gated_mix",
    )(xb, oa, og, w_gate, w_gate, w_pa, w_pg)


def _mm_nt_kernel(x_ref, wt_ref, o_ref):
    o_ref[...] = lax.dot_general(x_ref[...], wt_ref[...].astype(BF16), _NT,
                                 preferred_element_type=F32).astype(o_ref.dtype)


def _matmul_nt(x, wt, layer, ncols, out_dtype, name):
    m, k = x.shape
    tm = _row_tile(m, 2048)
    tn = _col_tile(ncols, 512)
    return pl.pallas_call(
        _mm_nt_kernel,
        out_shape=jax.ShapeDtypeStruct((m, ncols), out_dtype),
        grid=(m // tm, ncols // tn),
        in_specs=[_resident((tm, k), lambda i, j: (i, 0)),
                  pl.BlockSpec((None, tn, k), lambda i, j: (layer, j, 0))],
        out_specs=pl.BlockSpec((tm, tn), lambda i, j: (i, j)),
        compiler_params=_params("parallel", "parallel"),
        name=name,
    )(x, wt)


def _mm_res_kernel(a_ref, w_ref, r_ref, o_ref, *, alpha):
    o_ref[...] = alpha * r_ref[...] + _dot(a_ref[...], w_ref[...].astype(BF16))


def _matmul_residual(a, w, layer, res, alpha):
    m, k = a.shape
    n = w.shape[2]
    tm = _row_tile(m, 2048)
    tn = _col_tile(n, 256)
    return pl.pallas_call(
        functools.partial(_mm_res_kernel, alpha=alpha),
        out_shape=jax.ShapeDtypeStruct((m, n), F32),
        grid=(m // tm, n // tn),
        in_specs=[_resident((tm, k), lambda i, j: (i, 0)), _wspec(k, tn, layer),
                  pl.BlockSpec((tm, tn), lambda i, j: (i, j))],
        out_specs=pl.BlockSpec((tm, tn), lambda i, j: (i, j)),
        compiler_params=_params("parallel", "parallel"),
        name="out_proj_residual",
    )(a, w, res)


def _ffn_out_kernel(act_ref, wd_ref, xb_ref, wg_ref, p_ref, wp_ref, r_ref, o_ref, *, alpha):
    f = _dot(act_ref[...], wd_ref[...])
    ple = jax.nn.sigmoid(_dot(xb_ref[...], wg_ref[...])) * _dot(p_ref[...], wp_ref[...])
    o_ref[...] = (alpha * r_ref[...] + f) + ple


def _ffn_out(act, w_down, xb, w_pgate, pb, w_pproj, layer, res, alpha):
    m, f = act.shape
    d = w_down.shape[2]
    pd = pb.shape[2]
    tm = _row_tile(m, 512)
    tn = _col_tile(d, 256)
    return pl.pallas_call(
        functools.partial(_ffn_out_kernel, alpha=alpha),
        out_shape=jax.ShapeDtypeStruct((m, d), F32),
        grid=(m // tm, d // tn),
        in_specs=[pl.BlockSpec((tm, f), lambda i, j: (i, 0)), _wspec(f, tn, layer),
                  pl.BlockSpec((tm, d), lambda i, j: (i, 0)), _wspec(d, tn, layer),
                  pl.BlockSpec((None, tm, pd), lambda i, j: (layer, i, 0)), _wspec(pd, tn, layer),
                  pl.BlockSpec((tm, tn), lambda i, j: (i, j))],
        out_specs=pl.BlockSpec((tm, tn), lambda i, j: (i, j)),
        compiler_params=_params("parallel", "parallel"),
        name="ffn_out",
    )(act, w_down, xb, w_pgate, pb, w_pproj, res)


def _ln_kernel(s_ref, g_ref, b_ref, o_ref, ob_ref):
    x = s_ref[...]
    mu = jnp.mean(x, axis=-1, keepdims=True)
    xc = x - mu
    var = jnp.mean(xc * xc, axis=-1, keepdims=True)
    y = xc * lax.rsqrt(var + LN_EPS) * g_ref[...] + b_ref[...]
    o_ref[...] = y
    ob_ref[...] = y.astype(ob_ref.dtype)


def _layer_norm(s, g, b):
    m, d = s.shape
    tm = _row_tile(m, 256)
    return pl.pallas_call(
        _ln_kernel,
        out_shape=(jax.ShapeDtypeStruct((m, d), F32), jax.ShapeDtypeStruct((m, d), BF16)),
        grid=(m // tm,),
        in_specs=[pl.BlockSpec((tm, d), lambda i: (i, 0)),
                  pl.BlockSpec((1, d), lambda i: (0, 0)),
                  pl.BlockSpec((1, d), lambda i: (0, 0))],
        out_specs=(pl.BlockSpec((tm, d), lambda i: (i, 0)), pl.BlockSpec((tm, d), lambda i: (i, 0))),
        compiler_params=_params("parallel"),
        name="layer_norm",
    )(s, g.reshape(1, d), b.reshape(1, d))


def _causal_conv(u, buf, w, cb):
    row = lax.broadcasted_iota(jnp.int32, u.shape, 0)
    u1 = jnp.where(row == 0, buf[1:2], pltpu.roll(u, 1, 0))
    u2 = jnp.where(row == 0, buf[0:1], jnp.where(row == 1, buf[1:2], pltpu.roll(u, 2, 0)))
    return cb + ((w[0:1] * u2 + w[1:2] * u1) + w[2:3] * u)


def _gelu_gate(c_gate, c_val):
    return (0.5 * c_gate * (1.0 + lax.erf(c_gate * (2.0 ** -0.5)))) * c_val


def _conv_gate_kernel(ug_ref, uv_ref, bg_ref, bv_ref, wg_ref, wv_ref, cg_ref, cv_ref, o_ref):
    c_gate = _causal_conv(ug_ref[...], bg_ref[...], wg_ref[...], cg_ref[...])
    c_val = _causal_conv(uv_ref[...], bv_ref[...], wv_ref[...], cv_ref[...])
    o_ref[...] = _gelu_gate(c_gate, c_val).astype(o_ref.dtype)


def _up_conv_kernel(x_ref, wg_ref, wv_ref, bg_ref, bv_ref, cwg_ref, cwv_ref, cg_ref, cv_ref,
                    o_ref, tg_ref, tv_ref):
    t = x_ref.shape[0]
    rc = UP_CONV_ROW_CHUNK if t % UP_CONV_ROW_CHUNK == 0 else t
    prev_g, prev_v = bg_ref[...], bv_ref[...]
    for r in range(t // rc):
        rows = slice(r * rc, (r + 1) * rc)
        u_gate = _dot(x_ref[rows, :], wg_ref[...].astype(BF16))
        u_val = _dot(x_ref[rows, :], wv_ref[...].astype(BF16))
        c_gate = _causal_conv(u_gate, prev_g, cwg_ref[...], cg_ref[...])
        c_val = _causal_conv(u_val, prev_v, cwv_ref[...], cv_ref[...])
        o_ref[rows, :] = _gelu_gate(c_gate, c_val).astype(o_ref.dtype)
        prev_g, prev_v = u_gate[rc - (CONV_W - 1):, :], u_val[rc - (CONV_W - 1):, :]
    tg_ref[...] = prev_g
    tv_ref[...] = prev_v


def _up_conv_gate(xb, w_up, layer, buf, conv_w, conv_b, bsz, t):
    d = xb.shape[1]
    f2 = w_up.shape[2]
    f = f2 // 2
    tc = _col_tile(f, 256)
    nj = f // tc
    cb = conv_b.reshape(1, f2)
    w_spec = lambda off: pl.BlockSpec((None, d, tc), lambda b, j: (layer, 0, off + j))
    b_spec = lambda off: pl.BlockSpec((None, CONV_W - 1, tc), lambda b, j: (b, 0, off + j))
    cw_spec = lambda off: pl.BlockSpec((CONV_W, tc), lambda b, j: (0, off + j))
    cb_spec = lambda off: pl.BlockSpec((1, tc), lambda b, j: (0, off + j))
    tail_spec = pl.BlockSpec((None, CONV_W - 1, tc), lambda b, j: (b, 0, j))
    act, tail_g, tail_v = pl.pallas_call(
        _up_conv_kernel,
        out_shape=(jax.ShapeDtypeStruct((bsz * t, f), BF16),
                   jax.ShapeDtypeStruct((bsz, CONV_W - 1, f), F32),
                   jax.ShapeDtypeStruct((bsz, CONV_W - 1, f), F32)),
        grid=(bsz, nj),
        in_specs=[_resident((t, d), lambda b, j: (b, 0)), w_spec(0), w_spec(nj), b_spec(0), b_spec(nj),
                  cw_spec(0), cw_spec(nj), cb_spec(0), cb_spec(nj)],
        out_specs=(pl.BlockSpec((t, tc), lambda b, j: (b, j)), tail_spec, tail_spec),
        compiler_params=_params("parallel", "parallel"),
        name="up_conv_gate",
    )(xb, w_up, w_up, buf, buf, conv_w, conv_w, cb, cb)
    return act, jnp.concatenate([tail_g, tail_v], axis=-1)


def _conv_gate(u3, buf, conv_w, conv_b):
    bsz, t, f2 = u3.shape
    f = f2 // 2
    tc = _col_tile(f, max(256, CONV_BLOCK_ELEMS // t))
    nj = f // tc
    cb = conv_b.reshape(1, f2)
    u_spec = lambda off: pl.BlockSpec((None, t, tc), lambda b, j: (b, 0, off + j))
    b_spec = lambda off: pl.BlockSpec((None, CONV_W - 1, tc), lambda b, j: (b, 0, off + j))
    w_spec = lambda off: pl.BlockSpec((CONV_W, tc), lambda b, j: (0, off + j))
    c_spec = lambda off: pl.BlockSpec((1, tc), lambda b, j: (0, off + j))
    return pl.pallas_call(
        _conv_gate_kernel,
        out_shape=jax.ShapeDtypeStruct((bsz, t, f), BF16),
        grid=(bsz, nj),
        in_specs=[u_spec(0), u_spec(nj), b_spec(0), b_spec(nj), w_spec(0), w_spec(nj), c_spec(0), c_spec(nj)],
        out_specs=pl.BlockSpec((None, t, tc), lambda b, j: (b, 0, j)),
        compiler_params=_params("parallel", "parallel"),
        name="conv_gate",
    )(u3, u3, buf, buf, conv_w, conv_w, cb, cb)


def _block_mean_kernel(k_ref, o_ref):
    o_ref[...] = jnp.sum(k_ref[...], axis=0, keepdims=True) * (1.0 / MOBA_BLOCK)


def _block_mean(hmat, da, col_block):
    nb = hmat.shape[0] // MOBA_BLOCK
    out = pl.pallas_call(
        _block_mean_kernel,
        out_shape=jax.ShapeDtypeStruct((nb, 1, da), F32),
        grid=(nb,),
        in_specs=[pl.BlockSpec((MOBA_BLOCK, da), lambda i: (i, col_block))],
        out_specs=pl.BlockSpec((None, 1, da), lambda i: (i, 0, 0)),
        compiler_params=_params("parallel"),
        name="block_mean",
    )(hmat)
    return out.reshape(nb, da)


def _moba_prompt_kernel(q_ref, k_ref, v_ref, km_ref, o_ref, *, nb, scale):
    t = pl.program_id(2)
    blk = MOBA_BLOCK
    row = lax.broadcasted_iota(jnp.int32, (blk, blk), 0)
    col = lax.broadcasted_iota(jnp.int32, (blk, blk), 1)

    def attend(c, hs):
        q = q_ref[:, hs].astype(BF16)
        off = c * blk
        s_own = lax.dot_general(q, k_ref[off:off + blk, hs].astype(BF16), _NT, preferred_element_type=F32) * scale
        s_own = jnp.where(col <= row, s_own, NEG)
        m = jnp.max(s_own, axis=1, keepdims=True)
        if c > 0:
            s_past = lax.dot_general(q, k_ref[0:off, hs].astype(BF16), _NT, preferred_element_type=F32) * scale
            if c > MOBA_TOPK:
                g = lax.dot_general(q, km_ref[:, hs].astype(BF16), _NT, preferred_element_type=F32)
                lane = lax.broadcasted_iota(jnp.int32, g.shape, 1)
                g = jnp.where(lane < c, g, -jnp.inf)
                pieces = []
                for n in range(c):
                    gn = g[:, n:n + 1]
                    ahead = jnp.where(g > gn, 1.0, jnp.where((g == gn) & (lane < n), 1.0, 0.0))
                    rank = jnp.sum(ahead, axis=1, keepdims=True)
                    pieces.append(s_past[:, n * blk:(n + 1) * blk] + jnp.where(rank < MOBA_TOPK, 0.0, NEG))
                s_past = jnp.concatenate(pieces, axis=1)
            m = jnp.maximum(m, jnp.max(s_past, axis=1, keepdims=True))
        p_own = jnp.exp(s_own - m)
        denom = jnp.sum(p_own, axis=1, keepdims=True)
        o = _dot(p_own.astype(BF16), v_ref[off:off + blk, hs].astype(BF16))
        if c > 0:
            p_past = jnp.exp(s_past - m)
            denom = denom + jnp.sum(p_past, axis=1, keepdims=True)
            o = o + _dot(p_past.astype(BF16), v_ref[0:off, hs].astype(BF16))
        return (o / denom).astype(o_ref.dtype)

    def attend_heads(c):
        outs = [attend(c, slice(j * HD_A, (j + 1) * HD_A)) for j in range(q_ref.shape[1] // HD_A)]
        o_ref[...] = outs[0] if len(outs) == 1 else jnp.concatenate(outs, axis=1)

    for c in range(nb):
        pl.when(t == c)(functools.partial(attend_heads, c))


def _moba_prompt(hmat, da, bsz, t):
    m = hmat.shape[0]
    assert t % MOBA_BLOCK == 0
    h = da // HD_A
    nb = t // MOBA_BLOCK
    kmean = _block_mean(hmat, da, 1)
    hps = MOBA_HEADS_PER_STEP if h % MOBA_HEADS_PER_STEP == 0 else 1
    hg, w = h // hps, hps * HD_A
    return pl.pallas_call(
        functools.partial(_moba_prompt_kernel, nb=nb, scale=HD_A ** -0.5),
        out_shape=jax.ShapeDtypeStruct((m, da), BF16),
        grid=(bsz, hg, nb),
        in_specs=[pl.BlockSpec((MOBA_BLOCK, w), lambda b, hh, tt: (b * nb + tt, hh)),
                  pl.BlockSpec((t, w), lambda b, hh, tt: (b, hg + hh)),
                  pl.BlockSpec((t, w), lambda b, hh, tt: (b, 2 * hg + hh)),
                  pl.BlockSpec((nb, w), lambda b, hh, tt: (b, hh))],
        out_specs=pl.BlockSpec((MOBA_BLOCK, w), lambda b, hh, tt: (b * nb + tt, hh)),
        compiler_params=_params("parallel", "parallel", "parallel"),
        name="moba_prompt",
    )(hmat, hmat, hmat, kmean)


def _page_mean_kernel(pt_ref, *refs):
    del pt_ref
    page_refs, o_ref = refs[:-1], refs[-1]
    nblk = o_ref.shape[0]
    ppb = len(page_refs) // nblk
    for n in range(nblk):
        total = jnp.sum(page_refs[n * ppb][...], axis=0)
        for r in page_refs[n * ppb + 1:(n + 1) * ppb]:
            total = total + jnp.sum(r[...], axis=0)
        o_ref[n] = total * (1.0 / MOBA_BLOCK)


def _page_block_mean(cache, page_table, layer):
    _, _, psz, h, hd = cache.shape
    bd, n_pages = page_table.shape
    ppb = MOBA_BLOCK // psz
    assert MOBA_BLOCK % psz == 0 and n_pages % ppb == 0
    nblk = n_pages // ppb
    bps = max(c for c in (4, 2, 1) if nblk % c == 0)
    pps = bps * ppb
    page_spec = lambda k: pl.BlockSpec(
        (None, None, psz, h, hd), lambda b, j, pt: (pt[b, pps * j + k], layer, 0, 0, 0))
    out = pl.pallas_call(
        _page_mean_kernel,
        out_shape=jax.ShapeDtypeStruct((bd, nblk, h, hd), F32),
        grid_spec=pltpu.PrefetchScalarGridSpec(
            num_scalar_prefetch=1,
            grid=(bd, nblk // bps),
            in_specs=[page_spec(k) for k in range(pps)],
            out_specs=pl.BlockSpec((None, bps, h, hd), lambda b, j, pt: (b, j, 0, 0))),
        compiler_params=_params("parallel", "parallel"),
        name="page_block_mean",
    )(page_table, *([cache] * pps))
    return out.reshape(bd, nblk, h * hd)


def _topk_kernel(q_ref, km_ref, o_ref, *, nblk):
    nheads, td = o_ref.shape[0], o_ref.shape[1]
    lane = lax.broadcasted_iota(jnp.int32, (td, nblk), 1).astype(F32)
    out_lane = lax.broadcasted_iota(jnp.int32, (td, LANES), 1)
    for hh in range(nheads):
        hs = slice(hh * HD_A, (hh + 1) * HD_A)
        s = lax.dot_general(q_ref[:, hs].astype(BF16), km_ref[:, hs].astype(BF16), _NT,
                            preferred_element_type=F32)
        out = jnp.zeros((td, LANES), jnp.int32)
        for r in range(MOBA_TOPK):
            mx = jnp.max(s, axis=1, keepdims=True)
            idx = jnp.min(jnp.where(s == mx, lane, float(nblk)), axis=1, keepdims=True)
            out = jnp.where(out_lane == r, idx.astype(jnp.int32), out)
            s = jnp.where(lane == idx, -jnp.inf, s)
        o_ref[hh] = out


def _moba_topk(q2d, kmean, bd, td):
    _, nblk, da = kmean.shape
    assert nblk >= MOBA_TOPK
    h = da // HD_A
    out = pl.pallas_call(
        functools.partial(_topk_kernel, nblk=nblk),
        out_shape=jax.ShapeDtypeStruct((bd, h, td, LANES), jnp.int32),
        grid=(bd,),
        in_specs=[pl.BlockSpec((td, da), lambda b: (b, 0)),
                  pl.BlockSpec((None, nblk, da), lambda b: (b, 0, 0))],
        out_specs=pl.BlockSpec((None, h, td, LANES), lambda b: (b, 0, 0, 0)),
        compiler_params=_params("parallel"),
        name="moba_topk",
    )(q2d, kmean)
    return out[..., :MOBA_TOPK]


def _moba_sample_kernel(pg_ref, q_ref, kn_ref, vn_ref, ck_hbm, cv_hbm, o_ref, kbuf, vbuf, sem,
                        *, layer, nheads, npg, scale):
    td = q_ref.shape[0]
    n = td * npg
    psz = kbuf.shape[2]
    step = pl.program_id(0) * nheads + pl.program_id(1)
    nsteps = pl.num_programs(0) * nheads
    slot = step % 2

    def page_copies(page, head, half, i):
        return (pltpu.make_async_copy(ck_hbm.at[page, layer, :, head, :], kbuf.at[half, i], sem.at[0, half]),
                pltpu.make_async_copy(cv_hbm.at[page, layer, :, head, :], vbuf.at[half, i], sem.at[1, half]))

    def start_step(s, half):
        head = s % nheads
        for i in range(n):
            for cp in page_copies(pg_ref[s * n + i], head, half, i):
                cp.start()

    @pl.when(step == 0)
    def _():
        start_step(step, slot)

    @pl.when(step + 1 < nsteps)
    def _():
        start_step(step + 1, 1 - slot)

    for i in range(n):
        for cp in page_copies(0, 0, slot, i):
            cp.wait()

    q = q_ref[...].astype(BF16)
    kc = kbuf[slot].reshape(n * psz, HD_A).astype(BF16)
    vc = vbuf[slot].reshape(n * psz, HD_A).astype(BF16)
    s_past = lax.dot_general(q, kc, _NT, preferred_element_type=F32) * scale
    prow = lax.broadcasted_iota(jnp.int32, s_past.shape, 0)
    pcol = lax.broadcasted_iota(jnp.int32, s_past.shape, 1)
    own = (pcol >= prow * (npg * psz)) & (pcol < (prow + 1) * (npg * psz))
    s_past = jnp.where(own, s_past, NEG)
    s_new = lax.dot_general(q, kn_ref[...].astype(BF16), _NT, preferred_element_type=F32) * scale
    row = lax.broadcasted_iota(jnp.int32, s_new.shape, 0)
    col = lax.broadcasted_iota(jnp.int32, s_new.shape, 1)
    s_new = jnp.where(col <= row, s_new, NEG)
    m = jnp.maximum(jnp.max(s_past, axis=1, keepdims=True), jnp.max(s_new, axis=1, keepdims=True))
    p_past = jnp.exp(s_past - m)
    p_new = jnp.exp(s_new - m)
    denom = jnp.sum(p_past, axis=1, keepdims=True) + jnp.sum(p_new, axis=1, keepdims=True)
    o_ref[...] = (_dot(p_past.astype(BF16), vc) + _dot(p_new.astype(BF16), vn_ref[...].astype(BF16))) / denom


def _moba_sample(hmat, cache_k, cache_v, page_table, layer, bd, td):
    _, _, psz, h, hd = cache_k.shape
    da = h * hd
    n_pages = page_table.shape[1]
    assert (n_pages * psz) % MOBA_BLOCK == 0 and td <= MOBA_BLOCK
    kmean = _page_block_mean(cache_k, page_table, layer)
    top = _moba_topk(hmat, kmean, bd, td)
    ppb = MOBA_BLOCK // psz
    npg = MOBA_TOPK * ppb
    logical = (top[..., None] * ppb + jnp.arange(ppb, dtype=jnp.int32)).reshape(bd, h * td * npg)
    pages = jnp.take_along_axis(page_table, logical, axis=1).reshape(-1)

    row_spec = lambda group: pl.BlockSpec((td, HD_A), lambda b, hh, pg: (b, group * h + hh))
    hbm_spec = pl.BlockSpec(memory_space=pl.ANY)
    return pl.pallas_call(
        functools.partial(_moba_sample_kernel, layer=layer, nheads=h, npg=npg, scale=HD_A ** -0.5),
        out_shape=jax.ShapeDtypeStruct((bd * td, da), F32),
        grid_spec=pltpu.PrefetchScalarGridSpec(
            num_scalar_prefetch=1,
            grid=(bd, h),
            in_specs=[row_spec(0), row_spec(1), row_spec(2), hbm_spec, hbm_spec],
            out_specs=pl.BlockSpec((td, HD_A), lambda b, hh, pg: (b, hh)),
            scratch_shapes=[pltpu.VMEM((2, td * npg, psz, HD_A), F32),
                            pltpu.VMEM((2, td * npg, psz, HD_A), F32),
                            pltpu.SemaphoreType.DMA((2, 2))]),
        compiler_params=_params("arbitrary", "arbitrary"),
        name="moba_sample",
    )(pages, hmat, hmat, hmat, cache_k, cache_v)


def _gla_kernel(q_ref, k_ref, v_ref, r_ref, glr_ref, w2_ref, b2_ref, ng_ref, s0_ref,
                o_ref, sn_ref, state, cum_scr, intra_scr, *, nc, hg, dk, dv):
    c = pl.program_id(1)
    csz = q_ref.shape[0]
    qscale = dk ** -0.5

    @pl.when(c == 0)
    def _():
        state[...] = s0_ref[...]

    z = _dot(glr_ref[...].astype(BF16), w2_ref[...]) + b2_ref[...]
    g = (jnp.minimum(z, 0.0) - jnp.log1p(jnp.exp(-jnp.abs(z)))) * (1.0 / GLA_TAU)
    row = lax.broadcasted_iota(jnp.int32, (csz, csz), 0)
    col = lax.broadcasted_iota(jnp.int32, (csz, csz), 1)
    causal = col <= row
    cum = jnp.dot(causal.astype(F32), g, precision=lax.Precision.HIGHEST, preferred_element_type=F32)
    ones = jnp.ones((csz, LANES), F32)

    factorable = jnp.min(cum[csz - 1:csz, :]) >= GLA_FACTORED_MIN_LOG_DECAY

    @pl.when(factorable)
    def _():
        for hh in range(hg):
            ks = slice(hh * dk, (hh + 1) * dk)
            vs = slice(hh * dv, (hh + 1) * dv)
            cum_h = cum[:, ks]
            qe = (q_ref[:, ks] * qscale * jnp.exp(cum_h)).astype(BF16)
            ke = (k_ref[:, ks] * jnp.exp(-cum_h)).astype(BF16)
            a = jnp.where(causal, lax.dot_general(qe, ke, _NT, preferred_element_type=F32), 0.0)
            intra_scr[:, vs] = _dot(a.astype(BF16), v_ref[:, vs].astype(BF16))

    @pl.when(jnp.logical_not(factorable))
    def _():
        cum_scr[...] = cum
        rid = lax.broadcasted_iota(jnp.int32, (csz, 1), 0)
        for hh in range(hg):
            ks = slice(hh * dk, (hh + 1) * dk)
            vs = slice(hh * dv, (hh + 1) * dv)

            def score_row(i, carry, ks=ks, vs=vs):
                decay = jnp.exp(jnp.where(rid <= i, cum_scr[pl.ds(i, 1), ks] - cum_scr[:, ks], -jnp.inf))
                qi = q_ref[pl.ds(i, 1), ks] * qscale
                a_i = jnp.sum(qi * k_ref[:, ks] * decay, axis=1, keepdims=True)
                a_i = a_i.astype(BF16).astype(F32)
                vf = v_ref[:, vs].astype(BF16).astype(F32)
                intra_scr[pl.ds(i, 1), vs] = jnp.sum(a_i * vf, axis=0, keepdims=True)
                return carry

            lax.fori_loop(0, csz, score_row, 0)

    for hh in range(hg):
        ks = slice(hh * dk, (hh + 1) * dk)
        vs = slice(hh * dv, (hh + 1) * dv)
        cum_h = cum[:, ks]
        last = cum_h[csz - 1:csz, :]
        k = k_ref[:, ks]
        vb = v_ref[:, vs].astype(BF16)
        s_prev = state[hh]

        qe = (q_ref[:, ks] * qscale * jnp.exp(cum_h)).astype(BF16)
        o = _dot(qe, s_prev.astype(BF16)) + intra_scr[:, vs]

        kd = (k * jnp.exp(last - cum_h)).astype(BF16)
        tot = lax.dot_general(g[:, ks], ones, _TN, precision=lax.Precision.HIGHEST,
                              preferred_element_type=F32)
        state[hh] = jnp.exp(tot[:, :1]) * s_prev + lax.dot_general(kd, vb, _TN, preferred_element_type=F32)

        mu = jnp.mean(o, axis=-1, keepdims=True)
        oc = o - mu
        var = jnp.mean(oc * oc, axis=-1, keepdims=True)
        on = oc * lax.rsqrt(var + LN_EPS) * ng_ref[...]
        r = r_ref[:, vs]
        o_ref[:, vs] = (on * (r * jax.nn.sigmoid(r))).astype(o_ref.dtype)

    @pl.when(c == nc - 1)
    def _():
        sn_ref[...] = state[...]


def _gla(hmat, col0, glr_pad, w2_pad, layer, b2, norm_g, s0, s0_layer, bsz, t, out_dtype):
    m = hmat.shape[0]
    hg, dk, dv = s0.shape[2:]
    dkt, dvt = hg * dk, hg * dv
    assert col0 % dkt == 0 and (col0 + 2 * dkt) % dvt == 0
    qcol, vcol = col0 // dkt, (col0 + 2 * dkt) // dvt
    csz = GLA_CHUNK if t % GLA_CHUNK == 0 else t
    nc = t // csz
    rows = lambda w, col=0: pl.BlockSpec((csz, w), lambda b, c: (b * nc + c, col))
    const = lambda shape: pl.BlockSpec(shape, lambda b, c: (0,) * len(shape))
    return pl.pallas_call(
        functools.partial(_gla_kernel, nc=nc, hg=hg, dk=dk, dv=dv),
        out_shape=(jax.ShapeDtypeStruct((m, dvt), out_dtype), jax.ShapeDtypeStruct((bsz, hg, dk, dv), F32)),
        grid=(bsz, nc),
        in_specs=[rows(dkt, qcol), rows(dkt, qcol + 1), rows(dvt, vcol), rows(dvt, vcol + 1), rows(LANES),
                  pl.BlockSpec((None, LANES, dkt), lambda b, c: (layer, 0, 0)), const((1, dkt)), const((1, dv)),
                  pl.BlockSpec((None, None, hg, dk, dv), lambda b, c: (b, s0_layer, 0, 0, 0))],
        out_specs=(rows(dvt), pl.BlockSpec((None, hg, dk, dv), lambda b, c: (b, 0, 0, 0))),
        scratch_shapes=[pltpu.VMEM((hg, dk, dv), F32), pltpu.VMEM((csz, dkt), F32), pltpu.VMEM((csz, dvt), F32)],
        compiler_params=_params("parallel", "arbitrary"),
        name="gla",
    )(hmat, hmat, hmat, hmat, glr_pad, w2_pad, b2.reshape(1, dkt), norm_g.reshape(1, dv), s0)


def _decoder_layer(x, xb, pb, bsz, t, w, i, moba_fn, s0, s0_layer, conv_buf, alpha, attn_dtype):
    d = x.shape[1]
    da = d // 2
    hmat = _matmul_nt(xb, w["w_in_t"], i, w["w_in_t"].shape[1] - GLA_RANK, F32, "in_proj")
    glr = _matmul_nt(xb, w["w_glr_t"], i, LANES, F32, "in_proj_glr")
    ka, va = hmat[:, da:2 * da], hmat[:, 2 * da:3 * da]

    o_a = moba_fn(hmat).astype(BF16)
    o_g, s_new = _gla(hmat, 3 * da, glr, w["w_gla_a2"], i, w["b_gla_a"][i], w["gla_norm_g"][i],
                      s0, s0_layer, bsz, t, attn_dtype)
    o_g = o_g.astype(BF16)

    mix = _gated_mix(xb, o_a, o_g, w["w_merge_gate"], w["w_proj_a"], w["w_proj_g"], i)
    x1, x1b = _layer_norm(_matmul_residual(mix, w["w_out"], i, x, alpha), w["ln1_g"][i], w["ln1_b"][i])

    if t % BF16_SUBLANES == 0:
        act, conv_new = _up_conv_gate(x1b, w["w_up"], i, conv_buf, w["conv_w"][i], w["conv_b"][i], bsz, t)
    else:
        u3 = _matmul(x1b, w["w_up"], i, 0, w["w_up"].shape[2], F32, "up_proj").reshape(bsz, t, -1)
        act = _conv_gate(u3, conv_buf, w["conv_w"][i], w["conv_b"][i]).reshape(bsz * t, -1)
        conv_new = u3[:, t - (CONV_W - 1):]
    s2 = _ffn_out(act, w["w_down"], x1b, w["ple_gate"], pb, w["ple_proj"], i, x1, alpha)
    x2, x2b = _layer_norm(s2, w["ln2_g"][i], w["ln2_b"][i])
    return x2, x2b, ka, va, s_new, conv_new


def kernel(x_prompt, x_sample, cache_k, cache_v, state_gla, state_conv, page_table, p_prompt, p_sample,
           ln1_g, ln1_b, ln2_g, ln2_b, w_in, w_gla_a2, b_gla_a, gla_norm_g, w_proj_a, w_proj_g,
           w_merge_gate, w_out, w_up, conv_w, conv_b, w_down, ple_proj, ple_gate):
    bp, tp, d = x_prompt.shape
    bd, td, _ = x_sample.shape
    depth = w_in.shape[0]
    _, _, psz, h_a, hd = cache_k.shape
    assert hd == HD_A and psz == PAGE_SIZE
    hg, dk, dv = state_gla.shape[2:]
    f2 = w_up.shape[2]
    alpha = (2 * depth) ** 0.25
    n_main = w_in.shape[2] - GLA_RANK

    s0_prompt = jnp.zeros((bp, 1, hg, dk, dv), F32)
    conv0_prompt = jnp.zeros((bp, CONV_W - 1, f2), F32)

    w_in_t = jnp.swapaxes(w_in, 1, 2)
    w = {
        "w_in_t": w_in_t,
        "w_glr_t": jnp.pad(w_in_t[:, n_main:, :], ((0, 0), (0, LANES - GLA_RANK), (0, 0))).astype(BF16),
        "w_gla_a2": jnp.pad(w_gla_a2, ((0, 0), (0, LANES - GLA_RANK), (0, 0))).astype(BF16),
        "b_gla_a": b_gla_a, "gla_norm_g": gla_norm_g,
        "w_proj_a": w_proj_a.astype(BF16), "w_proj_g": w_proj_g.astype(BF16),
        "w_merge_gate": w_merge_gate.astype(BF16), "w_out": w_out,
        "w_up": w_up, "conv_w": conv_w, "conv_b": conv_b,
        "w_down": w_down.astype(BF16), "ple_proj": ple_proj.astype(BF16), "ple_gate": ple_gate.astype(BF16),
        "ln1_g": ln1_g, "ln1_b": ln1_b, "ln2_g": ln2_g, "ln2_b": ln2_b,
    }
    pb_prompt = p_prompt.reshape(depth, bp * tp, -1).astype(BF16)
    pb_sample = p_sample.reshape(depth, bd * td, -1).astype(BF16)

    xp, xpb = x_prompt.reshape(bp * tp, d), x_prompt.reshape(bp * tp, d).astype(BF16)
    xs, xsb = x_sample.reshape(bd * td, d), x_sample.reshape(bd * td, d).astype(BF16)
    outs = [[] for _ in range(8)]
    for i in range(depth):
        moba_p = lambda hmat: _moba_prompt(hmat, h_a * hd, bp, tp)
        xp, xpb, kp, vp, sp, cp = _decoder_layer(
            xp, xpb, pb_prompt, bp, tp, w, i, moba_p, s0_prompt, 0, conv0_prompt, alpha, BF16)
        moba_s = lambda hmat: _moba_sample(hmat, cache_k, cache_v, page_table, i, bd, td)
        xs, xsb, kn, vn, sn, cn = _decoder_layer(
            xs, xsb, pb_sample, bd, td, w, i, moba_s, state_gla, i, state_conv[:, i], alpha, F32)
        for lst, val in zip(outs, (kp.reshape(bp, tp, h_a, hd), vp.reshape(bp, tp, h_a, hd), sp, cp,
                                   kn.reshape(bd, td, h_a, hd), vn.reshape(bd, td, h_a, hd), sn, cn)):
            lst.append(val)

    stacked = [jnp.stack(lst, axis=1) for lst in outs]
    return (xp.reshape(bp, tp, d), xs.reshape(bd, td, d), *stacked)
```

```python
import functools

import jax
import jax.numpy as jnp
from jax import lax
from jax.experimental import pallas as pl
from jax.experimental.pallas import tpu as pltpu

HD_A = 128
MOBA_BLOCK = 256
MOBA_TOPK = 3
PAGE_SIZE = 128
H_G = 4
GLA_RANK = 16
GLA_TAU = 16.0
GLA_CHUNK = 64
GLA_FACTORED_MIN_LOG_DECAY = -60.0
CONV_W = 3
LN_EPS = 1e-5

LANES = 128
BF16_SUBLANES = 16
V7X_VMEM_LIMIT_BYTES = 56 * 1024 * 1024
CONV_BLOCK_ELEMS = 512 * 1024
MOBA_HEADS_PER_STEP = 2
UP_CONV_ROW_CHUNK = 512

NEG = -1e30
BF16 = jnp.bfloat16
F32 = jnp.float32

_NT = (((1,), (1,)), ((), ()))
_TN = (((0,), (0,)), ((), ()))


def _params(*sem):
    return pltpu.CompilerParams(dimension_semantics=sem, vmem_limit_bytes=V7X_VMEM_LIMIT_BYTES)


def _row_tile(m, pref):
    t = min(m, pref)
    assert m % t == 0, (m, t)
    return t


def _col_tile(n, pref, *offsets):
    t = (min(n, pref) // LANES) * LANES
    while t > LANES and (n % t or any(o % t for o in offsets)):
        t -= LANES
    assert t >= LANES and n % t == 0 and not any(o % t for o in offsets), (n, pref, offsets)
    return t


def _dot(a, b):
    return jnp.dot(a, b, preferred_element_type=F32)


def _resident(shape, index_map):
    return pl.BlockSpec(shape, index_map, pipeline_mode=pl.Buffered(1))


def _mm_kernel(x_ref, w_ref, o_ref):
    o_ref[...] = _dot(x_ref[...], w_ref[...].astype(BF16)).astype(o_ref.dtype)


def _wspec(k, tn, layer, col_block0=0):
    return pl.BlockSpec((None, k, tn), lambda i, j: (layer, 0, col_block0 + j))


def _matmul(x, w, layer, col0, ncols, out_dtype, name):
    m, k = x.shape
    tm = _row_tile(m, 2048)
    tn = _col_tile(ncols, 512, col0)
    return pl.pallas_call(
        _mm_kernel,
        out_shape=jax.ShapeDtypeStruct((m, ncols), out_dtype),
        grid=(m // tm, ncols // tn),
        in_specs=[_resident((tm, k), lambda i, j: (i, 0)), _wspec(k, tn, layer, col0 // tn)],
        out_specs=pl.BlockSpec((tm, tn), lambda i, j: (i, j)),
        compiler_params=_params("parallel", "parallel"),
        name=name,
    )(x, w)


def _mix_kernel(x_ref, oa_ref, og_ref, wga_ref, wgg_ref, wpa_ref, wpg_ref, o_ref):
    x = x_ref[...]
    gate_a = jax.nn.sigmoid(_dot(x, wga_ref[...]))
    gate_g = jax.nn.sigmoid(_dot(x, wgg_ref[...]))
    mix = gate_a * _dot(oa_ref[...], wpa_ref[...]) + gate_g * _dot(og_ref[...], wpg_ref[...])
    o_ref[...] = mix.astype(o_ref.dtype)


def _gated_mix(xb, oa, og, w_gate, w_pa, w_pg, layer):
    m, d = xb.shape
    da, dv = oa.shape[1], og.shape[1]
    tm = _row_tile(m, 1024)
    tn = _col_tile(d, 256)
    nj = d // tn
    return pl.pallas_call(
        _mix_kernel,
        out_shape=jax.ShapeDtypeStruct((m, d), BF16),
        grid=(m // tm, nj),
        in_specs=[pl.BlockSpec((tm, d), lambda i, j: (i, 0)),
                  pl.BlockSpec((tm, da), lambda i, j: (i, 0)),
                  pl.BlockSpec((tm, dv), lambda i, j: (i, 0)),
                  _wspec(d, tn, layer), _wspec(d, tn, layer, nj), _wspec(da, tn, layer), _wspec(dv, tn, layer)],
        out_specs=pl.BlockSpec((tm, tn), lambda i, j: (i, j)),
        compiler_params=_params("parallel", "parallel"),
        name="gated_mix",
    )(xb, oa, og, w_gate, w_gate, w_pa, w_pg)


def _mm_nt_kernel(x_ref, wt_ref, o_ref):
    o_ref[...] = lax.dot_general(x_ref[...], wt_ref[...].astype(BF16), _NT,
                                 preferred_element_type=F32).astype(o_ref.dtype)


def _matmul_nt(x, wt, layer, ncols, out_dtype, name):
    m, k = x.shape
    tm = _row_tile(m, 2048)
    tn = _col_tile(ncols, 512)
    return pl.pallas_call(
        _mm_nt_kernel,
        out_shape=jax.ShapeDtypeStruct((m, ncols), out_dtype),
        grid=(m // tm, ncols // tn),
        in_specs=[_resident((tm, k), lambda i, j: (i, 0)),
                  pl.BlockSpec((None, tn, k), lambda i, j: (layer, j, 0))],
        out_specs=pl.BlockSpec((tm, tn), lambda i, j: (i, j)),
        compiler_params=_params("parallel", "parallel"),
        name=name,
    )(x, wt)


def _mm_res_kernel(a_ref, w_ref, r_ref, o_ref, *, alpha):
    o_ref[...] = alpha * r_ref[...] + _dot(a_ref[...], w_ref[...].astype(BF16))


def _matmul_residual(a, w, layer, res, alpha):
    m, k = a.shape
    n = w.shape[2]
    tm = _row_tile(m, 2048)
    tn = _col_tile(n, 256)
    return pl.pallas_call(
        functools.partial(_mm_res_kernel, alpha=alpha),
        out_shape=jax.ShapeDtypeStruct((m, n), F32),
        grid=(m // tm, n // tn),
        in_specs=[_resident((tm, k), lambda i, j: (i, 0)), _wspec(k, tn, layer),
                  pl.BlockSpec((tm, tn), lambda i, j: (i, j))],
        out_specs=pl.BlockSpec((tm, tn), lambda i, j: (i, j)),
        compiler_params=_params("parallel", "parallel"),
        name="out_proj_residual",
    )(a, w, res)


def _ffn_out_kernel(act_ref, wd_ref, xb_ref, wg_ref, p_ref, wp_ref, r_ref, o_ref, *, alpha):
    f = _dot(act_ref[...], wd_ref[...])
    ple = jax.nn.sigmoid(_dot(xb_ref[...], wg_ref[...])) * _dot(p_ref[...], wp_ref[...])
    o_ref[...] = (alpha * r_ref[...] + f) + ple


def _ffn_out(act, w_down, xb, w_pgate, pb, w_pproj, layer, res, alpha):
    m, f = act.shape
    d = w_down.shape[2]
    pd = pb.shape[2]
    tm = _row_tile(m, 512)
    tn = _col_tile(d, 256)
    return pl.pallas_call(
        functools.partial(_ffn_out_kernel, alpha=alpha),
        out_shape=jax.ShapeDtypeStruct((m, d), F32),
        grid=(m // tm, d // tn),
        in_specs=[pl.BlockSpec((tm, f), lambda i, j: (i, 0)), _wspec(f, tn, layer),
                  pl.BlockSpec((tm, d), lambda i, j: (i, 0)), _wspec(d, tn, layer),
                  pl.BlockSpec((None, tm, pd), lambda i, j: (layer, i, 0)), _wspec(pd, tn, layer),
                  pl.BlockSpec((tm, tn), lambda i, j: (i, j))],
        out_specs=pl.BlockSpec((tm, tn), lambda i, j: (i, j)),
        compiler_params=_params("parallel", "parallel"),
        name="ffn_out",
    )(act, w_down, xb, w_pgate, pb, w_pproj, res)


def _ln_kernel(s_ref, g_ref, b_ref, o_ref, ob_ref):
    x = s_ref[...]
    mu = jnp.mean(x, axis=-1, keepdims=True)
    xc = x - mu
    var = jnp.mean(xc * xc, axis=-1, keepdims=True)
    y = xc * lax.rsqrt(var + LN_EPS) * g_ref[...] + b_ref[...]
    o_ref[...] = y
    ob_ref[...] = y.astype(ob_ref.dtype)


def _layer_norm(s, g, b):
    m, d = s.shape
    tm = _row_tile(m, 256)
    return pl.pallas_call(
        _ln_kernel,
        out_shape=(jax.ShapeDtypeStruct((m, d), F32), jax.ShapeDtypeStruct((m, d), BF16)),
        grid=(m // tm,),
        in_specs=[pl.BlockSpec((tm, d), lambda i: (i, 0)),
                  pl.BlockSpec((1, d), lambda i: (0, 0)),
                  pl.BlockSpec((1, d), lambda i: (0, 0))],
        out_specs=(pl.BlockSpec((tm, d), lambda i: (i, 0)), pl.BlockSpec((tm, d), lambda i: (i, 0))),
        compiler_params=_params("parallel"),
        name="layer_norm",
    )(s, g.reshape(1, d), b.reshape(1, d))


def _causal_conv(u, buf, w, cb):
    row = lax.broadcasted_iota(jnp.int32, u.shape, 0)
    u1 = jnp.where(row == 0, buf[1:2], pltpu.roll(u, 1, 0))
    u2 = jnp.where(row == 0, buf[0:1], jnp.where(row == 1, buf[1:2], pltpu.roll(u, 2, 0)))
    return cb + ((w[0:1] * u2 + w[1:2] * u1) + w[2:3] * u)


def _gelu_gate(c_gate, c_val):
    return (0.5 * c_gate * (1.0 + lax.erf(c_gate * (2.0 ** -0.5)))) * c_val


def _conv_gate_kernel(ug_ref, uv_ref, bg_ref, bv_ref, wg_ref, wv_ref, cg_ref, cv_ref, o_ref):
    c_gate = _causal_conv(ug_ref[...], bg_ref[...], wg_ref[...], cg_ref[...])
    c_val = _causal_conv(uv_ref[...], bv_ref[...], wv_ref[...], cv_ref[...])
    o_ref[...] = _gelu_gate(c_gate, c_val).astype(o_ref.dtype)


def _up_conv_kernel(x_ref, wg_ref, wv_ref, bg_ref, bv_ref, cwg_ref, cwv_ref, cg_ref, cv_ref,
                    o_ref, tg_ref, tv_ref):
    t = x_ref.shape[0]
    rc = UP_CONV_ROW_CHUNK if t % UP_CONV_ROW_CHUNK == 0 else t
    prev_g, prev_v = bg_ref[...], bv_ref[...]
    for r in range(t // rc):
        rows = slice(r * rc, (r + 1) * rc)
        u_gate = _dot(x_ref[rows, :], wg_ref[...].astype(BF16))
        u_val = _dot(x_ref[rows, :], wv_ref[...].astype(BF16))
        c_gate = _causal_conv(u_gate, prev_g, cwg_ref[...], cg_ref[...])
        c_val = _causal_conv(u_val, prev_v, cwv_ref[...], cv_ref[...])
        o_ref[rows, :] = _gelu_gate(c_gate, c_val).astype(o_ref.dtype)
        prev_g, prev_v = u_gate[rc - (CONV_W - 1):, :], u_val[rc - (CONV_W - 1):, :]
    tg_ref[...] = prev_g
    tv_ref[...] = prev_v


def _up_conv_gate(xb, w_up, layer, buf, conv_w, conv_b, bsz, t):
    d = xb.shape[1]
    f2 = w_up.shape[2]
    f = f2 // 2
    tc = _col_tile(f, 256)
    nj = f // tc
    cb = conv_b.reshape(1, f2)
    w_spec = lambda off: pl.BlockSpec((None, d, tc), lambda b, j: (layer, 0, off + j))
    b_spec = lambda off: pl.BlockSpec((None, CONV_W - 1, tc), lambda b, j: (b, 0, off + j))
    cw_spec = lambda off: pl.BlockSpec((CONV_W, tc), lambda b, j: (0, off + j))
    cb_spec = lambda off: pl.BlockSpec((1, tc), lambda b, j: (0, off + j))
    tail_spec = pl.BlockSpec((None, CONV_W - 1, tc), lambda b, j: (b, 0, j))
    act, tail_g, tail_v = pl.pallas_call(
        _up_conv_kernel,
        out_shape=(jax.ShapeDtypeStruct((bsz * t, f), BF16),
                   jax.ShapeDtypeStruct((bsz, CONV_W - 1, f), F32),
                   jax.ShapeDtypeStruct((bsz, CONV_W - 1, f), F32)),
        grid=(bsz, nj),
        in_specs=[_resident((t, d), lambda b, j: (b, 0)), w_spec(0), w_spec(nj), b_spec(0), b_spec(nj),
                  cw_spec(0), cw_spec(nj), cb_spec(0), cb_spec(nj)],
        out_specs=(pl.BlockSpec((t, tc), lambda b, j: (b, j)), tail_spec, tail_spec),
        compiler_params=_params("parallel", "parallel"),
        name="up_conv_gate",
    )(xb, w_up, w_up, buf, buf, conv_w, conv_w, cb, cb)
    return act, jnp.concatenate([tail_g, tail_v], axis=-1)


def _conv_gate(u3, buf, conv_w, conv_b):
    bsz, t, f2 = u3.shape
    f = f2 // 2
    tc = _col_tile(f, max(256, CONV_BLOCK_ELEMS // t))
    nj = f // tc
    cb = conv_b.reshape(1, f2)
    u_spec = lambda off: pl.BlockSpec((None, t, tc), lambda b, j: (b, 0, off + j))
    b_spec = lambda off: pl.BlockSpec((None, CONV_W - 1, tc), lambda b, j: (b, 0, off + j))
    w_spec = lambda off: pl.BlockSpec((CONV_W, tc), lambda b, j: (0, off + j))
    c_spec = lambda off: pl.BlockSpec((1, tc), lambda b, j: (0, off + j))
    return pl.pallas_call(
        _conv_gate_kernel,
        out_shape=jax.ShapeDtypeStruct((bsz, t, f), BF16),
        grid=(bsz, nj),
        in_specs=[u_spec(0), u_spec(nj), b_spec(0), b_spec(nj), w_spec(0), w_spec(nj), c_spec(0), c_spec(nj)],
        out_specs=pl.BlockSpec((None, t, tc), lambda b, j: (b, 0, j)),
        compiler_params=_params("parallel", "parallel"),
        name="conv_gate",
    )(u3, u3, buf, buf, conv_w, conv_w, cb, cb)


def _block_mean_kernel(k_ref, o_ref):
    o_ref[...] = jnp.sum(k_ref[...], axis=0, keepdims=True) * (1.0 / MOBA_BLOCK)


def _block_mean(hmat, da, col_block):
    nb = hmat.shape[0] // MOBA_BLOCK
    out = pl.pallas_call(
        _block_mean_kernel,
        out_shape=jax.ShapeDtypeStruct((nb, 1, da), F32),
        grid=(nb,),
        in_specs=[pl.BlockSpec((MOBA_BLOCK, da), lambda i: (i, col_block))],
        out_specs=pl.BlockSpec((None, 1, da), lambda i: (i, 0, 0)),
        compiler_params=_params("parallel"),
        name="block_mean",
    )(hmat)
    return out.reshape(nb, da)


def _moba_prompt_kernel(q_ref, k_ref, v_ref, km_ref, o_ref, *, nb, scale):
    t = pl.program_id(2)
    blk = MOBA_BLOCK
    row = lax.broadcasted_iota(jnp.int32, (blk, blk), 0)
    col = lax.broadcasted_iota(jnp.int32, (blk, blk), 1)

    def attend(c, hs):
        q = q_ref[:, hs].astype(BF16)
        off = c * blk
        s_own = lax.dot_general(q, k_ref[off:off + blk, hs].astype(BF16), _NT, preferred_element_type=F32) * scale
        s_own = jnp.where(col <= row, s_own, NEG)
        m = jnp.max(s_own, axis=1, keepdims=True)
        if c > 0:
            s_past = lax.dot_general(q, k_ref[0:off, hs].astype(BF16), _NT, preferred_element_type=F32) * scale
            if c > MOBA_TOPK:
                g = lax.dot_general(q, km_ref[:, hs].astype(BF16), _NT, preferred_element_type=F32)
                lane = lax.broadcasted_iota(jnp.int32, g.shape, 1)
                g = jnp.where(lane < c, g, -jnp.inf)
                pieces = []
                for n in range(c):
                    gn = g[:, n:n + 1]
                    ahead = jnp.where(g > gn, 1.0, jnp.where((g == gn) & (lane < n), 1.0, 0.0))
                    rank = jnp.sum(ahead, axis=1, keepdims=True)
                    pieces.append(s_past[:, n * blk:(n + 1) * blk] + jnp.where(rank < MOBA_TOPK, 0.0, NEG))
                s_past = jnp.concatenate(pieces, axis=1)
            m = jnp.maximum(m, jnp.max(s_past, axis=1, keepdims=True))
        p_own = jnp.exp(s_own - m)
        denom = jnp.sum(p_own, axis=1, keepdims=True)
        o = _dot(p_own.astype(BF16), v_ref[off:off + blk, hs].astype(BF16))
        if c > 0:
            p_past = jnp.exp(s_past - m)
            denom = denom + jnp.sum(p_past, axis=1, keepdims=True)
            o = o + _dot(p_past.astype(BF16), v_ref[0:off, hs].astype(BF16))
        return (o / denom).astype(o_ref.dtype)

    def attend_heads(c):
        outs = [attend(c, slice(j * HD_A, (j + 1) * HD_A)) for j in range(q_ref.shape[1] // HD_A)]
        o_ref[...] = outs[0] if len(outs) == 1 else jnp.concatenate(outs, axis=1)

    for c in range(nb):
        pl.when(t == c)(functools.partial(attend_heads, c))


def _moba_prompt(hmat, da, bsz, t):
    m = hmat.shape[0]
    assert t % MOBA_BLOCK == 0
    h = da // HD_A
    nb = t // MOBA_BLOCK
    kmean = _block_mean(hmat, da, 1)
    hps = MOBA_HEADS_PER_STEP if h % MOBA_HEADS_PER_STEP == 0 else 1
    hg, w = h // hps, hps * HD_A
    return pl.pallas_call(
        functools.partial(_moba_prompt_kernel, nb=nb, scale=HD_A ** -0.5),
        out_shape=jax.ShapeDtypeStruct((m, da), BF16),
        grid=(bsz, hg, nb),
        in_specs=[pl.BlockSpec((MOBA_BLOCK, w), lambda b, hh, tt: (b * nb + tt, hh)),
                  pl.BlockSpec((t, w), lambda b, hh, tt: (b, hg + hh)),
                  pl.BlockSpec((t, w), lambda b, hh, tt: (b, 2 * hg + hh)),
                  pl.BlockSpec((nb, w), lambda b, hh, tt: (b, hh))],
        out_specs=pl.BlockSpec((MOBA_BLOCK, w), lambda b, hh, tt: (b * nb + tt, hh)),
        compiler_params=_params("parallel", "parallel", "parallel"),
        name="moba_prompt",
    )(hmat, hmat, hmat, kmean)


def _page_mean_kernel(pt_ref, *refs):
    del pt_ref
    page_refs, o_ref = refs[:-1], refs[-1]
    nblk = o_ref.shape[0]
    ppb = len(page_refs) // nblk
    for n in range(nblk):
        total = jnp.sum(page_refs[n * ppb][...], axis=0)
        for r in page_refs[n * ppb + 1:(n + 1) * ppb]:
            total = total + jnp.sum(r[...], axis=0)
        o_ref[n] = total * (1.0 / MOBA_BLOCK)


def _page_block_mean(cache, page_table, layer):
    _, _, psz, h, hd = cache.shape
    bd, n_pages = page_table.shape
    ppb = MOBA_BLOCK // psz
    assert MOBA_BLOCK % psz == 0 and n_pages % ppb == 0
    nblk = n_pages // ppb
    bps = max(c for c in (4, 2, 1) if nblk % c == 0)
    pps = bps * ppb
    page_spec = lambda k: pl.BlockSpec(
        (None, None, psz, h, hd), lambda b, j, pt: (pt[b, pps * j + k], layer, 0, 0, 0))
    out = pl.pallas_call(
        _page_mean_kernel,
        out_shape=jax.ShapeDtypeStruct((bd, nblk, h, hd), F32),
        grid_spec=pltpu.PrefetchScalarGridSpec(
            num_scalar_prefetch=1,
            grid=(bd, nblk // bps),
            in_specs=[page_spec(k) for k in range(pps)],
            out_specs=pl.BlockSpec((None, bps, h, hd), lambda b, j, pt: (b, j, 0, 0))),
        compiler_params=_params("parallel", "parallel"),
        name="page_block_mean",
    )(page_table, *([cache] * pps))
    return out.reshape(bd, nblk, h * hd)


def _topk_kernel(q_ref, km_ref, o_ref, *, nblk):
    nheads, td = o_ref.shape[0], o_ref.shape[1]
    lane = lax.broadcasted_iota(jnp.int32, (td, nblk), 1).astype(F32)
    out_lane = lax.broadcasted_iota(jnp.int32, (td, LANES), 1)
    for hh in range(nheads):
        hs = slice(hh * HD_A, (hh + 1) * HD_A)
        s = lax.dot_general(q_ref[:, hs].astype(BF16), km_ref[:, hs].astype(BF16), _NT,
                            preferred_element_type=F32)
        out = jnp.zeros((td, LANES), jnp.int32)
        for r in range(MOBA_TOPK):
            mx = jnp.max(s, axis=1, keepdims=True)
            idx = jnp.min(jnp.where(s == mx, lane, float(nblk)), axis=1, keepdims=True)
            out = jnp.where(out_lane == r, idx.astype(jnp.int32), out)
            s = jnp.where(lane == idx, -jnp.inf, s)
        o_ref[hh] = out


def _moba_topk(q2d, kmean, bd, td):
    _, nblk, da = kmean.shape
    assert nblk >= MOBA_TOPK
    h = da // HD_A
    out = pl.pallas_call(
        functools.partial(_topk_kernel, nblk=nblk),
        out_shape=jax.ShapeDtypeStruct((bd, h, td, LANES), jnp.int32),
        grid=(bd,),
        in_specs=[pl.BlockSpec((td, da), lambda b: (b, 0)),
                  pl.BlockSpec((None, nblk, da), lambda b: (b, 0, 0))],
        out_specs=pl.BlockSpec((None, h, td, LANES), lambda b: (b, 0, 0, 0)),
        compiler_params=_params("parallel"),
        name="moba_topk",
    )(q2d, kmean)
    return out[..., :MOBA_TOPK]


def _moba_sample_kernel(pg_ref, q_ref, kn_ref, vn_ref, ck_hbm, cv_hbm, o_ref, kbuf, vbuf, sem,
                        *, layer, nheads, npg, scale):
    td = q_ref.shape[0]
    n = td * npg
    psz = kbuf.shape[2]
    step = pl.program_id(0) * nheads + pl.program_id(1)
    nsteps = pl.num_programs(0) * nheads
    slot = step % 2

    def page_copies(page, head, half, i):
        return (pltpu.make_async_copy(ck_hbm.at[page, layer, :, head, :], kbuf.at[half, i], sem.at[0, half]),
                pltpu.make_async_copy(cv_hbm.at[page, layer, :, head, :], vbuf.at[half, i], sem.at[1, half]))

    def start_step(s, half):
        head = s % nheads
        for i in range(n):
            for cp in page_copies(pg_ref[s * n + i], head, half, i):
                cp.start()

    @pl.when(step == 0)
    def _():
        start_step(step, slot)

    @pl.when(step + 1 < nsteps)
    def _():
        start_step(step + 1, 1 - slot)

    for i in range(n):
        for cp in page_copies(0, 0, slot, i):
            cp.wait()

    q = q_ref[...].astype(BF16)
    kc = kbuf[slot].reshape(n * psz, HD_A).astype(BF16)
    vc = vbuf[slot].reshape(n * psz, HD_A).astype(BF16)
    s_past = lax.dot_general(q, kc, _NT, preferred_element_type=F32) * scale
    prow = lax.broadcasted_iota(jnp.int32, s_past.shape, 0)
    pcol = lax.broadcasted_iota(jnp.int32, s_past.shape, 1)
    own = (pcol >= prow * (npg * psz)) & (pcol < (prow + 1) * (npg * psz))
    s_past = jnp.where(own, s_past, NEG)
    s_new = lax.dot_general(q, kn_ref[...].astype(BF16), _NT, preferred_element_type=F32) * scale
    row = lax.broadcasted_iota(jnp.int32, s_new.shape, 0)
    col = lax.broadcasted_iota(jnp.int32, s_new.shape, 1)
    s_new = jnp.where(col <= row, s_new, NEG)
    m = jnp.maximum(jnp.max(s_past, axis=1, keepdims=True), jnp.max(s_new, axis=1, keepdims=True))
    p_past = jnp.exp(s_past - m)
    p_new = jnp.exp(s_new - m)
    denom = jnp.sum(p_past, axis=1, keepdims=True) + jnp.sum(p_new, axis=1, keepdims=True)
    o_ref[...] = (_dot(p_past.astype(BF16), vc) + _dot(p_new.astype(BF16), vn_ref[...].astype(BF16))) / denom


def _moba_sample(hmat, cache_k, cache_v, page_table, layer, bd, td):
    _, _, psz, h, hd = cache_k.shape
    da = h * hd
    n_pages = page_table.shape[1]
    assert (n_pages * psz) % MOBA_BLOCK == 0 and td <= MOBA_BLOCK
    kmean = _page_block_mean(cache_k, page_table, layer)
    top = _moba_topk(hmat, kmean, bd, td)
    ppb = MOBA_BLOCK // psz
    npg = MOBA_TOPK * ppb
    logical = (top[..., None] * ppb + jnp.arange(ppb, dtype=jnp.int32)).reshape(bd, h * td * npg)
    pages = jnp.take_along_axis(page_table, logical, axis=1).reshape(-1)

    row_spec = lambda group: pl.BlockSpec((td, HD_A), lambda b, hh, pg: (b, group * h + hh))
    hbm_spec = pl.BlockSpec(memory_space=pl.ANY)
    return pl.pallas_call(
        functools.partial(_moba_sample_kernel, layer=layer, nheads=h, npg=npg, scale=HD_A ** -0.5),
        out_shape=jax.ShapeDtypeStruct((bd * td, da), F32),
        grid_spec=pltpu.PrefetchScalarGridSpec(
            num_scalar_prefetch=1,
            grid=(bd, h),
            in_specs=[row_spec(0), row_spec(1), row_spec(2), hbm_spec, hbm_spec],
            out_specs=pl.BlockSpec((td, HD_A), lambda b, hh, pg: (b, hh)),
            scratch_shapes=[pltpu.VMEM((2, td * npg, psz, HD_A), F32),
                            pltpu.VMEM((2, td * npg, psz, HD_A), F32),
                            pltpu.SemaphoreType.DMA((2, 2))]),
        compiler_params=_params("arbitrary", "arbitrary"),
        name="moba_sample",
    )(pages, hmat, hmat, hmat, cache_k, cache_v)


def _gla_kernel(q_ref, k_ref, v_ref, r_ref, glr_ref, w2_ref, b2_ref, ng_ref, s0_ref,
                o_ref, sn_ref, state, cum_scr, inter_scr, intra_scr, *, nc, hg, dk, dv):
    c = pl.program_id(1)
    csz = q_ref.shape[0]
    qscale = dk ** -0.5

    @pl.when(c == 0)
    def _():
        state[...] = s0_ref[...]

    z = _dot(glr_ref[...].astype(BF16), w2_ref[...]) + b2_ref[...]
    g = (jnp.minimum(z, 0.0) - jnp.log1p(jnp.exp(-jnp.abs(z)))) * (1.0 / GLA_TAU)
    row = lax.broadcasted_iota(jnp.int32, (csz, csz), 0)
    col = lax.broadcasted_iota(jnp.int32, (csz, csz), 1)
    causal = col <= row
    cum = jnp.dot(causal.astype(F32), g, precision=lax.Precision.HIGHEST, preferred_element_type=F32)
    ones = jnp.ones((csz, LANES), F32)

    def write_out(o, vs):
        mu = jnp.mean(o, axis=-1, keepdims=True)
        oc = o - mu
        var = jnp.mean(oc * oc, axis=-1, keepdims=True)
        on = oc * lax.rsqrt(var + LN_EPS) * ng_ref[...]
        r = r_ref[:, vs]
        o_ref[:, vs] = (on * (r * jax.nn.sigmoid(r))).astype(o_ref.dtype)

    cap = -GLA_FACTORED_MIN_LOG_DECAY
    for hh in range(hg):
        ks = slice(hh * dk, (hh + 1) * dk)
        vs = slice(hh * dv, (hh + 1) * dv)
        cum_h = cum[:, ks]
        last = cum_h[csz - 1:csz, :]
        k = k_ref[:, ks]
        vb = v_ref[:, vs].astype(BF16)
        s_prev = state[hh]

        qe = (q_ref[:, ks] * qscale * jnp.exp(cum_h)).astype(BF16)
        ke = (k * jnp.exp(jnp.minimum(-cum_h, cap))).astype(BF16)
        a = jnp.where(causal, lax.dot_general(qe, ke, _NT, preferred_element_type=F32), 0.0)
        inter = _dot(qe, s_prev.astype(BF16))
        inter_scr[:, vs] = inter
        write_out(inter + _dot(a.astype(BF16), vb), vs)

        kd = (k * jnp.exp(last - cum_h)).astype(BF16)
        tot = lax.dot_general(g[:, ks], ones, _TN, precision=lax.Precision.HIGHEST,
                              preferred_element_type=F32)
        state[hh] = jnp.exp(tot[:, :1]) * s_prev + lax.dot_general(kd, vb, _TN, preferred_element_type=F32)

    @pl.when(jnp.min(cum[csz - 1:csz, :]) < GLA_FACTORED_MIN_LOG_DECAY)
    def _():
        cum_scr[...] = cum
        rid = lax.broadcasted_iota(jnp.int32, (csz, 1), 0)
        for hh in range(hg):
            ks = slice(hh * dk, (hh + 1) * dk)
            vs = slice(hh * dv, (hh + 1) * dv)

            def score_row(i, carry, ks=ks, vs=vs):
                decay = jnp.exp(jnp.where(rid <= i, cum_scr[pl.ds(i, 1), ks] - cum_scr[:, ks], -jnp.inf))
                qi = q_ref[pl.ds(i, 1), ks] * qscale
                a_i = jnp.sum(qi * k_ref[:, ks] * decay, axis=1, keepdims=True)
                a_i = a_i.astype(BF16).astype(F32)
                vf = v_ref[:, vs].astype(BF16).astype(F32)
                intra_scr[pl.ds(i, 1), vs] = jnp.sum(a_i * vf, axis=0, keepdims=True)
                return carry

            lax.fori_loop(0, csz, score_row, 0)
            write_out(inter_scr[:, vs] + intra_scr[:, vs], vs)

    @pl.when(c == nc - 1)
    def _():
        sn_ref[...] = state[...]


def _gla(hmat, col0, glr_pad, w2_pad, layer, b2, norm_g, s0, s0_layer, bsz, t, out_dtype):
    m = hmat.shape[0]
    hg, dk, dv = s0.shape[2:]
    dkt, dvt = hg * dk, hg * dv
    assert col0 % dkt == 0 and (col0 + 2 * dkt) % dvt == 0
    qcol, vcol = col0 // dkt, (col0 + 2 * dkt) // dvt
    csz = GLA_CHUNK if t % GLA_CHUNK == 0 else t
    nc = t // csz
    rows = lambda w, col=0: pl.BlockSpec((csz, w), lambda b, c: (b * nc + c, col))
    const = lambda shape: pl.BlockSpec(shape, lambda b, c: (0,) * len(shape))
    return pl.pallas_call(
        functools.partial(_gla_kernel, nc=nc, hg=hg, dk=dk, dv=dv),
        out_shape=(jax.ShapeDtypeStruct((m, dvt), out_dtype), jax.ShapeDtypeStruct((bsz, hg, dk, dv), F32)),
        grid=(bsz, nc),
        in_specs=[rows(dkt, qcol), rows(dkt, qcol + 1), rows(dvt, vcol), rows(dvt, vcol + 1), rows(LANES),
                  pl.BlockSpec((None, LANES, dkt), lambda b, c: (layer, 0, 0)), const((1, dkt)), const((1, dv)),
                  pl.BlockSpec((None, None, hg, dk, dv), lambda b, c: (b, s0_layer, 0, 0, 0))],
        out_specs=(rows(dvt), pl.BlockSpec((None, hg, dk, dv), lambda b, c: (b, 0, 0, 0))),
        scratch_shapes=[pltpu.VMEM((hg, dk, dv), F32), pltpu.VMEM((csz, dkt), F32),
                        pltpu.VMEM((csz, dvt), F32), pltpu.VMEM((csz, dvt), F32)],
        compiler_params=_params("parallel", "arbitrary"),
        name="gla",
    )(hmat, hmat, hmat, hmat, glr_pad, w2_pad, b2.reshape(1, dkt), norm_g.reshape(1, dv), s0)


def _decoder_layer(x, xb, pb, bsz, t, w, i, moba_fn, s0, s0_layer, conv_buf, alpha, attn_dtype):
    d = x.shape[1]
    da = d // 2
    hmat = _matmul_nt(xb, w["w_in_t"], i, w["w_in_t"].shape[1] - GLA_RANK, F32, "in_proj")
    glr = _matmul_nt(xb, w["w_glr_t"], i, LANES, F32, "in_proj_glr")
    ka, va = hmat[:, da:2 * da], hmat[:, 2 * da:3 * da]

    o_a = moba_fn(hmat).astype(BF16)
    o_g, s_new = _gla(hmat, 3 * da, glr, w["w_gla_a2"], i, w["b_gla_a"][i], w["gla_norm_g"][i],
                      s0, s0_layer, bsz, t, attn_dtype)
    o_g = o_g.astype(BF16)

    mix = _gated_mix(xb, o_a, o_g, w["w_merge_gate"], w["w_proj_a"], w["w_proj_g"], i)
    x1, x1b = _layer_norm(_matmul_residual(mix, w["w_out"], i, x, alpha), w["ln1_g"][i], w["ln1_b"][i])

    if t % BF16_SUBLANES == 0:
        act, conv_new = _up_conv_gate(x1b, w["w_up"], i, conv_buf, w["conv_w"][i], w["conv_b"][i], bsz, t)
    else:
        u3 = _matmul(x1b, w["w_up"], i, 0, w["w_up"].shape[2], F32, "up_proj").reshape(bsz, t, -1)
        act = _conv_gate(u3, conv_buf, w["conv_w"][i], w["conv_b"][i]).reshape(bsz * t, -1)
        conv_new = u3[:, t - (CONV_W - 1):]
    s2 = _ffn_out(act, w["w_down"], x1b, w["ple_gate"], pb, w["ple_proj"], i, x1, alpha)
    x2, x2b = _layer_norm(s2, w["ln2_g"][i], w["ln2_b"][i])
    return x2, x2b, ka, va, s_new, conv_new


def kernel(x_prompt, x_sample, cache_k, cache_v, state_gla, state_conv, page_table, p_prompt, p_sample,
           ln1_g, ln1_b, ln2_g, ln2_b, w_in, w_gla_a2, b_gla_a, gla_norm_g, w_proj_a, w_proj_g,
           w_merge_gate, w_out, w_up, conv_w, conv_b, w_down, ple_proj, ple_gate):
    bp, tp, d = x_prompt.shape
    bd, td, _ = x_sample.shape
    depth = w_in.shape[0]
    _, _, psz, h_a, hd = cache_k.shape
    assert hd == HD_A and psz == PAGE_SIZE
    hg, dk, dv = state_gla.shape[2:]
    f2 = w_up.shape[2]
    alpha = (2 * depth) ** 0.25
    n_main = w_in.shape[2] - GLA_RANK

    s0_prompt = jnp.zeros((bp, 1, hg, dk, dv), F32)
    conv0_prompt = jnp.zeros((bp, CONV_W - 1, f2), F32)

    w_in_t = jnp.swapaxes(w_in, 1, 2)
    w = {
        "w_in_t": w_in_t,
        "w_glr_t": jnp.pad(w_in_t[:, n_main:, :], ((0, 0), (0, LANES - GLA_RANK), (0, 0))).astype(BF16),
        "w_gla_a2": jnp.pad(w_gla_a2, ((0, 0), (0, LANES - GLA_RANK), (0, 0))).astype(BF16),
        "b_gla_a": b_gla_a, "gla_norm_g": gla_norm_g,
        "w_proj_a": w_proj_a.astype(BF16), "w_proj_g": w_proj_g.astype(BF16),
        "w_merge_gate": w_merge_gate.astype(BF16), "w_out": w_out,
        "w_up": w_up, "conv_w": conv_w, "conv_b": conv_b,
        "w_down": w_down.astype(BF16), "ple_proj": ple_proj.astype(BF16), "ple_gate": ple_gate.astype(BF16),
        "ln1_g": ln1_g, "ln1_b": ln1_b, "ln2_g": ln2_g, "ln2_b": ln2_b,
    }
    pb_prompt = p_prompt.reshape(depth, bp * tp, -1).astype(BF16)
    pb_sample = p_sample.reshape(depth, bd * td, -1).astype(BF16)

    xp, xpb = x_prompt.reshape(bp * tp, d), x_prompt.reshape(bp * tp, d).astype(BF16)
    xs, xsb = x_sample.reshape(bd * td, d), x_sample.reshape(bd * td, d).astype(BF16)
    outs = [[] for _ in range(8)]
    for i in range(depth):
        moba_p = lambda hmat: _moba_prompt(hmat, h_a * hd, bp, tp)
        xp, xpb, kp, vp, sp, cp = _decoder_layer(
            xp, xpb, pb_prompt, bp, tp, w, i, moba_p, s0_prompt, 0, conv0_prompt, alpha, BF16)
        moba_s = lambda hmat: _moba_sample(hmat, cache_k, cache_v, page_table, i, bd, td)
        xs, xsb, kn, vn, sn, cn = _decoder_layer(
            xs, xsb, pb_sample, bd, td, w, i, moba_s, state_gla, i, state_conv[:, i], alpha, F32)
        for lst, val in zip(outs, (kp.reshape(bp, tp, h_a, hd), vp.reshape(bp, tp, h_a, hd), sp, cp,
                                   kn.reshape(bd, td, h_a, hd), vn.reshape(bd, td, h_a, hd), sn, cn)):
            lst.append(val)

    stacked = [jnp.stack(lst, axis=1) for lst in outs]
    return (xp.reshape(bp, tp, d), xs.reshape(bd, td, d), *stacked)
```
